```python
import jax, jax.numpy as jnp
from jax import lax
import numpy as np

D_MODEL = 1024
BATCH = 2
SEQ = 8192
DEPTH = 1

N_GROUPS_A = 8
GROUP_DIM_A = 64
WIDTH_A = N_GROUPS_A * GROUP_DIM_A
CHUNK = 128
N_HEADS_B = 8
HEAD_DIM_B = 64
WIDTH_B = N_HEADS_B * HEAD_DIM_B
MOBA_BLOCK = 256
MOBA_TOPK = 3
Q_BLOCK = 64
ROPE_THETA = 10000.0
D_FF = 2816
CONV_WIDTH = 3
DEEPNORM_ALPHA = (2.0 * DEPTH) ** 0.25
DEEPNORM_BETA = (8.0 * DEPTH) ** -0.25
LN_EPS = 1e-5
PROJ_WIDTH = 2 * WIDTH_A + 3 * WIDTH_B + 2 * D_MODEL

kernel_name = "hybrid_sgu_moba_convffn_deepnorm"


def layer_norm(x, g, b):
    xf = x.astype(jnp.float32)
    mu = jnp.mean(xf, axis=-1, keepdims=True)
    xc = xf - mu
    var = jnp.mean(jnp.square(xc), axis=-1, keepdims=True)
    y = xc * lax.rsqrt(var + LN_EPS) * g.astype(jnp.float32) + b.astype(jnp.float32)
    return y.astype(x.dtype)


def rope(x):
    s, dh = x.shape[1], x.shape[3]
    half = dh // 2
    inv_freq = ROPE_THETA ** (-jnp.arange(half, dtype=jnp.float32) / half)
    ang = jnp.arange(s, dtype=jnp.float32)[:, None] * inv_freq[None, :]
    cos = jnp.cos(ang)[None, :, None, :]
    sin = jnp.sin(ang)[None, :, None, :]
    xf = x.astype(jnp.float32)
    x1, x2 = xf[..., :half], xf[..., half:]
    out = jnp.concatenate([x1 * cos - x2 * sin, x2 * cos + x1 * sin], axis=-1)
    return out.astype(x.dtype)


def spatial_gating(u, v, ln_g, ln_b, w_s, b_s):
    bsz, s, _ = u.shape
    nc = s // CHUNK
    vn = layer_norm(v, ln_g, ln_b).reshape(bsz, nc, CHUNK, N_GROUPS_A, GROUP_DIM_A)
    w_causal = jnp.tril(w_s)
    mixed = jnp.einsum('gts,bcsgd->bctgd', w_causal, vn) + b_s.T[:, :, None]
    return u * mixed.reshape(bsz, s, WIDTH_A)


def moba_attention(q, k, v):
    bsz, s, h, dh = q.shape
    nb = -(-s // MOBA_BLOCK)
    pad = nb * MOBA_BLOCK - s
    qh = q.transpose(0, 2, 1, 3)
    kh = jnp.pad(k.transpose(0, 2, 1, 3), ((0, 0), (0, 0), (0, pad), (0, 0)))
    vh = jnp.pad(v.transpose(0, 2, 1, 3), ((0, 0), (0, 0), (0, pad), (0, 0)))
    k_blocks = kh.reshape(bsz, h, nb, MOBA_BLOCK, dh)
    v_blocks = vh.reshape(bsz, h, nb, MOBA_BLOCK, dh)
    k_mean = jnp.mean(k_blocks.astype(jnp.float32), axis=3)
    n_sel = min(MOBA_TOPK, nb)
    scale = dh ** -0.5
    b_idx = jnp.arange(bsz)[:, None, None, None]
    h_idx = jnp.arange(h)[None, :, None, None]

    def one_block(qb):
        q0 = qb * Q_BLOCK
        q_blk = lax.dynamic_slice_in_dim(qh, q0, Q_BLOCK, axis=2)
        own = q0 // MOBA_BLOCK
        q_pos = q0 + jnp.arange(Q_BLOCK)
        gate = jnp.einsum('bhqd,bhnd->bhqn', q_blk.astype(jnp.float32), k_mean)
        gate = jnp.where(jnp.arange(nb) < own, gate, -jnp.inf)
        _, idx = lax.top_k(gate, n_sel)
        sel_ok = jnp.arange(n_sel) < own
        k_sel = k_blocks[b_idx, h_idx, idx]
        v_sel = v_blocks[b_idx, h_idx, idx]
        s_sel = jnp.einsum('bhqd,bhqnkd->bhqnk', q_blk, k_sel).astype(jnp.float32) * scale
        s_sel = jnp.where(sel_ok[:, None], s_sel, -jnp.inf).reshape(bsz, h, Q_BLOCK, n_sel * MOBA_BLOCK)
        k_own = lax.dynamic_slice_in_dim(kh, own * MOBA_BLOCK, MOBA_BLOCK, axis=2)
        v_own = lax.dynamic_slice_in_dim(vh, own * MOBA_BLOCK, MOBA_BLOCK, axis=2)
        k_pos = own * MOBA_BLOCK + jnp.arange(MOBA_BLOCK)
        s_own = jnp.einsum('bhqd,bhkd->bhqk', q_blk, k_own).astype(jnp.float32) * scale
        s_own = jnp.where(k_pos[None, :] <= q_pos[:, None], s_own, -jnp.inf)
        p = jax.nn.softmax(jnp.concatenate([s_sel, s_own], axis=-1), axis=-1).astype(v.dtype)
        p_sel = p[..., :n_sel * MOBA_BLOCK].reshape(bsz, h, Q_BLOCK, n_sel, MOBA_BLOCK)
        p_own = p[..., n_sel * MOBA_BLOCK:]
        return (jnp.einsum('bhqnk,bhqnkd->bhqd', p_sel, v_sel)
                + jnp.einsum('bhqk,bhkd->bhqd', p_own, v_own))

    outs = lax.map(one_block, jnp.arange(s // Q_BLOCK))
    return outs.transpose(1, 0, 3, 2, 4).reshape(bsz, s, h * dh)


def token_mixer(x, w_in, b_gate, sgu_ln_g, sgu_ln_b, w_spatial, b_spatial,
                w_branch_a, w_branch_b, w_out):
    bsz, s, _ = x.shape
    proj = x @ w_in
    cuts = [WIDTH_A, 2 * WIDTH_A, 2 * WIDTH_A + WIDTH_B, 2 * WIDTH_A + 2 * WIDTH_B,
            2 * WIDTH_A + 3 * WIDTH_B, 2 * WIDTH_A + 3 * WIDTH_B + D_MODEL]
    u_a, v_a, q_b, k_b, v_b, g_a, g_b = jnp.split(proj, cuts, axis=-1)
    y_a = spatial_gating(jax.nn.gelu(u_a, approximate=False), jax.nn.gelu(v_a, approximate=False),
                         sgu_ln_g, sgu_ln_b, w_spatial, b_spatial)
    q_b = rope(q_b.reshape(bsz, s, N_HEADS_B, HEAD_DIM_B))
    k_b = rope(k_b.reshape(bsz, s, N_HEADS_B, HEAD_DIM_B))
    v_b = v_b.reshape(bsz, s, N_HEADS_B, HEAD_DIM_B)
    y_b = moba_attention(q_b, k_b, v_b)
    gate_a = jax.nn.sigmoid(g_a + b_gate[:D_MODEL])
    gate_b = jax.nn.sigmoid(g_b + b_gate[D_MODEL:])
    merged = gate_a * (y_a @ w_branch_a) + gate_b * (y_b @ w_branch_b)
    return merged @ w_out


def conv_ffn(x, w_up, conv_w, conv_b, w_down):
    h = x @ w_up
    c = h.shape[-1]
    h = lax.conv_general_dilated(h, conv_w[:, None, :], window_strides=(1,),
                                 padding=[(CONV_WIDTH - 1, 0)],
                                 dimension_numbers=('NWC', 'WIO', 'NWC'),
                                 feature_group_count=c) + conv_b
    a, g = jnp.split(h, 2, axis=-1)
    return (jax.nn.gelu(a, approximate=False) * g) @ w_down


def setup_inputs(seed: int = 0) -> dict:
    key = jax.random.key(seed)
    ks = jax.random.split(key, 20)
    f32 = jnp.float32
    nrm = lambda k, shape, sc: jax.random.normal(k, shape, f32) * sc
    L = DEPTH
    return {
        "x": nrm(ks[0], (BATCH, SEQ, D_MODEL), 1.0),
        "w_in": nrm(ks[1], (L, D_MODEL, PROJ_WIDTH), D_MODEL ** -0.5),
        "b_gate": nrm(ks[2], (L, 2 * D_MODEL), 0.01),
        "sgu_ln_g": 1.0 + nrm(ks[3], (L, WIDTH_A), 0.01),
        "sgu_ln_b": nrm(ks[4], (L, WIDTH_A), 0.01),
        "w_spatial": nrm(ks[5], (L, N_GROUPS_A, CHUNK, CHUNK), CHUNK ** -0.5),
        "b_spatial": 1.0 + nrm(ks[6], (L, N_GROUPS_A, CHUNK), 0.01),
        "w_branch_a": nrm(ks[7], (L, WIDTH_A, D_MODEL), WIDTH_A ** -0.5 * DEEPNORM_BETA),
        "w_branch_b": nrm(ks[8], (L, WIDTH_B, D_MODEL), WIDTH_B ** -0.5 * DEEPNORM_BETA),
        "w_out": nrm(ks[9], (L, D_MODEL, D_MODEL), D_MODEL ** -0.5 * DEEPNORM_BETA),
        "ln1_g": 1.0 + nrm(ks[10], (L, D_MODEL), 0.01),
        "ln1_b": nrm(ks[11], (L, D_MODEL), 0.01),
        "w_up": nrm(ks[12], (L, D_MODEL, 2 * D_FF), D_MODEL ** -0.5),
        "conv_w": nrm(ks[13], (L, CONV_WIDTH, 2 * D_FF), CONV_WIDTH ** -0.5),
        "conv_b": nrm(ks[14], (L, 2 * D_FF), 0.01),
        "w_down": nrm(ks[15], (L, D_FF, D_MODEL), D_FF ** -0.5 * DEEPNORM_BETA),
        "ln2_g": 1.0 + nrm(ks[16], (L, D_MODEL), 0.01),
        "ln2_b": nrm(ks[17], (L, D_MODEL), 0.01),
    }


def reference(x, w_in, b_gate, sgu_ln_g, sgu_ln_b, w_spatial, b_spatial, w_branch_a,
              w_branch_b, w_out, ln1_g, ln1_b, w_up, conv_w, conv_b, w_down, ln2_g, ln2_b):
    for l in range(DEPTH):
        mix = token_mixer(x, w_in[l], b_gate[l], sgu_ln_g[l], sgu_ln_b[l], w_spatial[l],
                          b_spatial[l], w_branch_a[l], w_branch_b[l], w_out[l])
        x = layer_norm(DEEPNORM_ALPHA * x + mix, ln1_g[l], ln1_b[l])
        ffn = conv_ffn(x, w_up[l], conv_w[l], conv_b[l], w_down[l])
        x = layer_norm(DEEPNORM_ALPHA * x + ffn, ln2_g[l], ln2_b[l])
    return x
```

```python
import functools

import jax
import jax.numpy as jnp
import numpy as np
from jax import lax
from jax.experimental import pallas as pl
from jax.experimental.pallas import tpu as pltpu

F32 = jnp.float32
BF16 = jnp.bfloat16

D_MODEL = 1024
N_GROUPS_A = 8
GROUP_DIM_A = 64
WIDTH_A = N_GROUPS_A * GROUP_DIM_A
CHUNK = 128
N_HEADS_B = 8
HEAD_DIM_B = 64
WIDTH_B = N_HEADS_B * HEAD_DIM_B
MOBA_BLOCK = 256
MOBA_TOPK = 3
ROPE_THETA = 10000.0
D_FF = 2816
CONV_WIDTH = 3
LN_EPS = 1e-5
PROJ_WIDTH = 2 * WIDTH_A + 3 * WIDTH_B + 2 * D_MODEL

LANES = 128
ROW_TILE = 256
FF_TILE = 256
V_EXT_ROWS = HEAD_DIM_B + 16
VMEM_LIMIT = 56 * 1024 * 1024
NEG_INF = float("-inf")


def _gelu(x):
    return 0.5 * x * (1.0 + lax.erf(x * np.float32(np.sqrt(0.5))))


def _layer_norm(x, g, b):
    mu = jnp.mean(x, axis=-1, keepdims=True)
    xc = x - mu
    var = jnp.mean(xc * xc, axis=-1, keepdims=True)
    return xc * lax.rsqrt(var + LN_EPS) * g + b


def _dot(a, b):
    return jnp.dot(a, b, preferred_element_type=F32)


def _dot_nt(a, b):
    return lax.dot_general(a, b, (((1,), (1,)), ((), ())), preferred_element_type=F32)


def _dot_tn(a, b):
    return lax.dot_general(a, b, (((0,), (0,)), ((), ())), preferred_element_type=F32)


def _rope(t, cos, sin_signed):
    lane = lax.broadcasted_iota(jnp.int32, (t.shape[0], LANES), 1)
    first_half = (lane % HEAD_DIM_B) < (HEAD_DIM_B // 2)
    out = []
    for c in range(t.shape[1] // LANES):
        blk = t[:, c * LANES:(c + 1) * LANES]
        up = pltpu.roll(blk, LANES - HEAD_DIM_B // 2, 1)
        down = pltpu.roll(blk, HEAD_DIM_B // 2, 1)
        partner = jnp.where(first_half, up, down)
        out.append(blk * cos + partner * sin_signed)
    return jnp.concatenate(out, axis=1)


def _proj_kernel(x_ref, w_ref, wvt_ref, cos_ref, sin_ref, lng_ref, lnb_ref, wsp_ref, bsp_ref,
                 ya_ref, q_ref, k_ref, vt_ref, g_ref):
    xb = x_ref[...].astype(BF16)
    gu = _gelu(_dot(xb, w_ref[:, 0:WIDTH_A]))
    gv = _gelu(_dot(xb, w_ref[:, WIDTH_A:2 * WIDTH_A]))
    vn = _layer_norm(gv, lng_ref[...], lnb_ref[...]).astype(BF16)

    t_idx = lax.broadcasted_iota(jnp.int32, (CHUNK, 2 * CHUNK), 0)
    s_idx = lax.broadcasted_iota(jnp.int32, (CHUNK, 2 * CHUNK), 1) % CHUNK
    causal = s_idx <= t_idx
    lane = lax.broadcasted_iota(jnp.int32, (CHUNK, LANES), 1)
    left = lane < GROUP_DIM_A
    zero = jnp.zeros((CHUNK, LANES), BF16)
    for c in range(ROW_TILE // CHUNK):
        rows = slice(c * CHUNK, (c + 1) * CHUNK)
        mixed = []
        for p in range(N_GROUPS_A // 2):
            blk = vn[rows, p * LANES:(p + 1) * LANES]
            stacked = jnp.concatenate([jnp.where(left, blk, zero), jnp.where(left, zero, blk)], axis=0)
            w_pair = jnp.where(causal, wsp_ref[p], jnp.zeros((), BF16))
            mixed.append(_dot(w_pair, stacked))
        mixed = jnp.concatenate(mixed, axis=1) + bsp_ref[...]
        ya_ref[rows, :] = (gu[rows, :] * mixed).astype(BF16)

    cos = cos_ref[...]
    sin = sin_ref[...]
    q = _rope(_dot(xb, w_ref[:, 2 * WIDTH_A:2 * WIDTH_A + WIDTH_B]), cos, sin)
    q_ref[...] = (q * np.float32(HEAD_DIM_B ** -0.5)).astype(BF16)
    k = _rope(_dot(xb, w_ref[:, 2 * WIDTH_A + WIDTH_B:2 * WIDTH_A + 2 * WIDTH_B]), cos, sin)
    k_ref[...] = k.astype(BF16)
    vt_ref[0, 0] = _dot_nt(wvt_ref[...], xb).astype(BF16)
    g_ref[...] = _dot(xb, w_ref[:, 2 * WIDTH_A + 3 * WIDTH_B:])


def _projection(x2, w_in_b, w_vt, cos_t, sin_t, ln_g, ln_b, w_sp, b_sp, bsz, seq):
    n_rows = x2.shape[0]
    tiles_per_seq = seq // ROW_TILE
    const = lambda i: (0, 0)
    return pl.pallas_call(
        _proj_kernel,
        grid=(n_rows // ROW_TILE,),
        in_specs=[
            pl.BlockSpec((ROW_TILE, D_MODEL), lambda i: (i, 0)),
            pl.BlockSpec((D_MODEL, PROJ_WIDTH), const),
            pl.BlockSpec((WIDTH_B, D_MODEL), const),
            pl.BlockSpec((ROW_TILE, LANES), lambda i: (i % tiles_per_seq, 0)),
            pl.BlockSpec((ROW_TILE, LANES), lambda i: (i % tiles_per_seq, 0)),
            pl.BlockSpec((1, WIDTH_A), const),
            pl.BlockSpec((1, WIDTH_A), const),
            pl.BlockSpec((N_GROUPS_A // 2, CHUNK, 2 * CHUNK), lambda i: (0, 0, 0)),
            pl.BlockSpec((CHUNK, WIDTH_A), const),
        ],
        out_specs=[
            pl.BlockSpec((ROW_TILE, WIDTH_A), lambda i: (i, 0)),
            pl.BlockSpec((ROW_TILE, WIDTH_B), lambda i: (i, 0)),
            pl.BlockSpec((ROW_TILE, WIDTH_B), lambda i: (i, 0)),
            pl.BlockSpec((1, 1, WIDTH_B, ROW_TILE),
                         lambda i: (i // tiles_per_seq, i % tiles_per_seq, 0, 0)),
            pl.BlockSpec((ROW_TILE, 2 * D_MODEL), lambda i: (i, 0)),
        ],
        out_shape=[
            jax.ShapeDtypeStruct((n_rows, WIDTH_A), BF16),
            jax.ShapeDtypeStruct((n_rows, WIDTH_B), BF16),
            jax.ShapeDtypeStruct((n_rows, WIDTH_B), BF16),
            jax.ShapeDtypeStruct((bsz, tiles_per_seq, WIDTH_B, ROW_TILE), BF16),
            jax.ShapeDtypeStruct((n_rows, 2 * D_MODEL), F32),
        ],
        compiler_params=pltpu.CompilerParams(
            dimension_semantics=("arbitrary",), vmem_limit_bytes=VMEM_LIMIT),
        name="sgu_moba_projection",
    )(x2, w_in_b, w_vt, cos_t, sin_t, ln_g, ln_b, w_sp, b_sp)


def _attn_kernel(q_ref, k_ref, vt_ref, o_ref, vext_ref, bias_ref, *, n_blocks):
    heads = LANES // HEAD_DIM_B
    for h in range(heads):
        vext_ref[h, :, 0:HEAD_DIM_B, :] = vt_ref[0, :, h * HEAD_DIM_B:(h + 1) * HEAD_DIM_B, :]
        vext_ref[h, :, HEAD_DIM_B:, :] = jnp.ones(
            (n_blocks, V_EXT_ROWS - HEAD_DIM_B, MOBA_BLOCK), BF16)

    k_sum = jnp.sum(k_ref[0].astype(F32).reshape(n_blocks, MOBA_BLOCK, LANES), axis=1)
    k_mean = k_sum * np.float32(1.0 / MOBA_BLOCK)
    lane_k = lax.broadcasted_iota(jnp.int32, (n_blocks, LANES), 1)
    lane_q = lax.broadcasted_iota(jnp.int32, (MOBA_BLOCK, LANES), 1)
    blk_row = lax.broadcasted_iota(jnp.int32, (n_blocks, MOBA_BLOCK), 0)
    key_pos = lax.broadcasted_iota(jnp.int32, (MOBA_BLOCK, MOBA_BLOCK), 0)
    qry_pos = lax.broadcasted_iota(jnp.int32, (MOBA_BLOCK, MOBA_BLOCK), 1)

    def q_tile(i, carry):
        row0 = pl.multiple_of(i * MOBA_BLOCK, MOBA_BLOCK)
        q_t = q_ref[0, pl.ds(row0, MOBA_BLOCK), :]
        k_own = k_ref[0, pl.ds(row0, MOBA_BLOCK), :]
        for h in range(heads):
            qm = jnp.where(lane_q // HEAD_DIM_B == h, q_t, jnp.zeros((), BF16))
            km = jnp.where(lane_k // HEAD_DIM_B == h, k_mean, 0.0)
            km_hi = km.astype(BF16)
            km_lo = (km - km_hi.astype(F32)).astype(BF16)
            gate = _dot_nt(km_hi, qm) + _dot_nt(km_lo, qm)

            past = blk_row < i
            gate = jnp.where(past, gate, NEG_INF)
            sel = jnp.zeros(gate.shape, jnp.bool_)
            for _ in range(MOBA_TOPK):
                best = jnp.max(gate, axis=0, keepdims=True)
                first = jnp.min(jnp.where(gate == best, blk_row, n_blocks), axis=0, keepdims=True)
                pick = blk_row == first
                sel = sel | pick
                gate = jnp.where(pick, NEG_INF, gate)
            bias_ref[h] = jnp.where(sel & past, 0.0, NEG_INF).astype(F32)

            s = _dot_nt(k_own, qm)
            s = jnp.where(key_pos <= qry_pos, s, NEG_INF)
            m0 = jnp.max(s, axis=0, keepdims=True)
            p = jnp.exp(s - m0).astype(BF16)
            acc0 = _dot(vext_ref[h, i], p)

            def kv_block(j, mc, h=h, qm=qm):
                m_run, acc = mc
                col0 = pl.multiple_of(j * MOBA_BLOCK, MOBA_BLOCK)
                k_j = k_ref[0, pl.ds(col0, MOBA_BLOCK), :]
                s_j = _dot_nt(k_j, qm) + bias_ref[h, pl.ds(j, 1), :]
                m_new = jnp.maximum(m_run, jnp.max(s_j, axis=0, keepdims=True))
                alpha = jnp.exp(m_run - m_new)
                p_j = jnp.exp(s_j - m_new).astype(BF16)
                return m_new, alpha * acc + _dot(vext_ref[h, j], p_j)

            _, acc = lax.fori_loop(0, i, kv_block, (m0, acc0))
            out = acc[0:HEAD_DIM_B, :] / acc[HEAD_DIM_B:HEAD_DIM_B + 1, :]
            o_ref[0, 0, i, h * HEAD_DIM_B:(h + 1) * HEAD_DIM_B, :] = out.astype(BF16)
        return carry

    lax.fori_loop(0, n_blocks, q_tile, 0)


def _moba_attention(q, k, vt, bsz, seq):
    n_blocks = seq // MOBA_BLOCK
    pairs = WIDTH_B // LANES
    q3 = q.reshape(bsz, seq, WIDTH_B)
    k3 = k.reshape(bsz, seq, WIDTH_B)
    heads = LANES // HEAD_DIM_B
    return pl.pallas_call(
        functools.partial(_attn_kernel, n_blocks=n_blocks),
        grid=(bsz, pairs),
        in_specs=[
            pl.BlockSpec((1, seq, LANES), lambda b, p: (b, 0, p)),
            pl.BlockSpec((1, seq, LANES), lambda b, p: (b, 0, p)),
            pl.BlockSpec((1, n_blocks, LANES, MOBA_BLOCK), lambda b, p: (b, 0, p, 0)),
        ],
        out_specs=pl.BlockSpec((1, 1, n_blocks, LANES, MOBA_BLOCK), lambda b, p: (b, p, 0, 0, 0)),
        out_shape=jax.ShapeDtypeStruct((bsz, pairs, n_blocks, LANES, MOBA_BLOCK), BF16),
        scratch_shapes=[
            pltpu.VMEM((heads, n_blocks, V_EXT_ROWS, MOBA_BLOCK), BF16),
            pltpu.VMEM((heads, n_blocks, MOBA_BLOCK), F32),
        ],
        compiler_params=pltpu.CompilerParams(
            dimension_semantics=("arbitrary", "arbitrary"), vmem_limit_bytes=VMEM_LIMIT),
        name="moba_attention",
    )(q3, k3, vt)


def _merge_kernel(x_ref, ya_ref, ybt_ref, g_ref, bg_ref, wa_ref, wb_ref, wo_ref, lng_ref, lnb_ref,
                  o_ref, *, alpha):
    branch_a = _dot(ya_ref[...], wa_ref[...])
    ybt = ybt_ref[0, :, 0].reshape(WIDTH_B, ROW_TILE)
    branch_b = _dot_tn(ybt, wb_ref[...])
    g = g_ref[...] + bg_ref[...]
    merged = (jax.nn.sigmoid(g[:, :D_MODEL]) * branch_a
              + jax.nn.sigmoid(g[:, D_MODEL:]) * branch_b)
    mix = _dot(merged.astype(BF16), wo_ref[...])
    o_ref[...] = _layer_norm(alpha * x_ref[...] + mix, lng_ref[...], lnb_ref[...])


def _merge(x2, ya, ybt, g, b_gate, w_a, w_b, w_o, ln_g, ln_b, alpha, seq):
    n_rows = x2.shape[0]
    tiles_per_seq = seq // ROW_TILE
    pairs = WIDTH_B // LANES
    const = lambda i: (0, 0)
    return pl.pallas_call(
        functools.partial(_merge_kernel, alpha=alpha),
        grid=(n_rows // ROW_TILE,),
        in_specs=[
            pl.BlockSpec((ROW_TILE, D_MODEL), lambda i: (i, 0)),
            pl.BlockSpec((ROW_TILE, WIDTH_A), lambda i: (i, 0)),
            pl.BlockSpec((1, pairs, 1, LANES, MOBA_BLOCK),
                         lambda i: (i // tiles_per_seq, 0, i % tiles_per_seq, 0, 0)),
            pl.BlockSpec((ROW_TILE, 2 * D_MODEL), lambda i: (i, 0)),
            pl.BlockSpec((1, 2 * D_MODEL), const),
            pl.BlockSpec((WIDTH_A, D_MODEL), const),
            pl.BlockSpec((WIDTH_B, D_MODEL), const),
            pl.BlockSpec((D_MODEL, D_MODEL), const),
            pl.BlockSpec((1, D_MODEL), const),
            pl.BlockSpec((1, D_MODEL), const),
        ],
        out_specs=pl.BlockSpec((ROW_TILE, D_MODEL), lambda i: (i, 0)),
        out_shape=jax.ShapeDtypeStruct((n_rows, D_MODEL), F32),
        compiler_params=pltpu.CompilerParams(
            dimension_semantics=("arbitrary",), vmem_limit_bytes=VMEM_LIMIT),
        name="merge_outproj_ln",
    )(x2, ya, ybt, g, b_gate, w_a, w_b, w_o, ln_g, ln_b)


def _ffn_kernel(x_ref, wup_ref, cw_ref, cb_ref, wdn_ref, lng_ref, lnb_ref, o_ref,
                hbuf_ref, carry_ref, acc_ref, *, alpha, tiles_per_seq):
    i = pl.program_id(0)

    @pl.when(i % tiles_per_seq == 0)
    def _():
        carry_ref[...] = jnp.zeros(carry_ref.shape, F32)

    x = x_ref[...]
    xb = x.astype(BF16)
    acc_ref[...] = jnp.zeros(acc_ref.shape, F32)
    pad = carry_ref.shape[0]
    for j in range(D_FF // FF_TILE):
        conv = []
        for half in range(2):
            c0 = half * D_FF + j * FF_TILE
            cols = slice(c0, c0 + FF_TILE)
            hbuf_ref[0:pad, :] = carry_ref[:, cols]
            h = _dot(xb, wup_ref[:, cols])
            hbuf_ref[pad:, :] = h
            carry_ref[:, cols] = h[ROW_TILE - pad:, :]
            cw = cw_ref[:, cols]
            conv.append(cw[0:1, :] * hbuf_ref[pad - 2:pad - 2 + ROW_TILE, :]
                        + cw[1:2, :] * hbuf_ref[pad - 1:pad - 1 + ROW_TILE, :]
                        + cw[2:3, :] * h + cb_ref[:, cols])
        act = (_gelu(conv[0]) * conv[1]).astype(BF16)
        acc_ref[...] += _dot(act, wdn_ref[j * FF_TILE:(j + 1) * FF_TILE, :])
    o_ref[...] = _layer_norm(alpha * x + acc_ref[...], lng_ref[...], lnb_ref[...])


def _conv_ffn(x1, w_up, conv_w, conv_b, w_dn, ln_g, ln_b, alpha, seq):
    n_rows = x1.shape[0]
    tiles_per_seq = seq // ROW_TILE
    const = lambda i: (0, 0)
    pad = 8
    return pl.pallas_call(
        functools.partial(_ffn_kernel, alpha=alpha, tiles_per_seq=tiles_per_seq),
        grid=(n_rows // ROW_TILE,),
        in_specs=[
            pl.BlockSpec((ROW_TILE, D_MODEL), lambda i: (i, 0)),
            pl.BlockSpec((D_MODEL, 2 * D_FF), const),
            pl.BlockSpec((CONV_WIDTH, 2 * D_FF), const),
            pl.BlockSpec((1, 2 * D_FF), const),
            pl.BlockSpec((D_FF, D_MODEL), const),
            pl.BlockSpec((1, D_MODEL), const),
            pl.BlockSpec((1, D_MODEL), const),
        ],
        out_specs=pl.BlockSpec((ROW_TILE, D_MODEL), lambda i: (i, 0)),
        out_shape=jax.ShapeDtypeStruct((n_rows, D_MODEL), F32),
        scratch_shapes=[
            pltpu.VMEM((pad + ROW_TILE, FF_TILE), F32),
            pltpu.VMEM((pad, 2 * D_FF), F32),
            pltpu.VMEM((ROW_TILE, D_MODEL), F32),
        ],
        compiler_params=pltpu.CompilerParams(
            dimension_semantics=("arbitrary",), vmem_limit_bytes=VMEM_LIMIT),
        name="conv_ffn_ln",
    )(x1, w_up, conv_w, conv_b, w_dn, ln_g, ln_b)


def _rope_tables(seq):
    half = HEAD_DIM_B // 2
    inv_freq = ROPE_THETA ** (-jnp.arange(half, dtype=F32) / half)
    ang = jnp.arange(seq, dtype=F32)[:, None] * inv_freq[None, :]
    cos = jnp.tile(jnp.cos(ang), (1, LANES // half))
    sin = jnp.tile(jnp.sin(ang), (1, LANES // half))
    first_half = (jnp.arange(LANES) % HEAD_DIM_B) < half
    return cos, jnp.where(first_half[None, :], -sin, sin)


def kernel(x, w_in, b_gate, sgu_ln_g, sgu_ln_b, w_spatial, b_spatial, w_branch_a, w_branch_b,
           w_out, ln1_g, ln1_b, w_up, conv_w, conv_b, w_down, ln2_g, ln2_b):
    bsz, seq, d_model = x.shape
    depth = w_in.shape[0]
    assert d_model == D_MODEL and seq % ROW_TILE == 0 and ROW_TILE == MOBA_BLOCK
    alpha = float((2.0 * depth) ** 0.25)
    cos_t, sin_t = _rope_tables(seq)
    x2 = x.reshape(bsz * seq, d_model)
    for l in range(depth):
        w_in_b = w_in[l].astype(BF16)
        w_vt = w_in_b[:, 2 * WIDTH_A + 2 * WIDTH_B:2 * WIDTH_A + 3 * WIDTH_B].T
        w_sp = w_spatial[l].astype(BF16).reshape(N_GROUPS_A // 2, 2, CHUNK, CHUNK)
        w_sp = w_sp.transpose(0, 2, 1, 3).reshape(N_GROUPS_A // 2, CHUNK, 2 * CHUNK)
        b_sp = jnp.repeat(b_spatial[l].T, GROUP_DIM_A, axis=1)
        ya, q, k, vt, g = _projection(
            x2, w_in_b, w_vt, cos_t, sin_t, sgu_ln_g[l][None, :], sgu_ln_b[l][None, :],
            w_sp, b_sp, bsz, seq)
        ybt = _moba_attention(q, k, vt, bsz, seq)
        x1 = _merge(x2, ya, ybt, g, b_gate[l][None, :], w_branch_a[l].astype(BF16),
                    w_branch_b[l].astype(BF16), w_out[l].astype(BF16),
                    ln1_g[l][None, :], ln1_b[l][None, :], alpha, seq)
        x2 = _conv_ffn(x1, w_up[l].astype(BF16), conv_w[l], conv_b[l][None, :],
                       w_down[l].astype(BF16), ln2_g[l][None, :], ln2_b[l][None, :], alpha, seq)
    return x2.reshape(bsz, seq, d_model)
```

```python
import functools

import jax
import jax.numpy as jnp
import numpy as np
from jax import lax
from jax.experimental import pallas as pl
from jax.experimental.pallas import tpu as pltpu

F32 = jnp.float32
BF16 = jnp.bfloat16

D_MODEL = 1024
N_GROUPS_A = 8
GROUP_DIM_A = 64
WIDTH_A = N_GROUPS_A * GROUP_DIM_A
CHUNK = 128
N_HEADS_B = 8
HEAD_DIM_B = 64
WIDTH_B = N_HEADS_B * HEAD_DIM_B
MOBA_BLOCK = 256
MOBA_TOPK = 3
ROPE_THETA = 10000.0
D_FF = 2816
CONV_WIDTH = 3
LN_EPS = 1e-5
PROJ_WIDTH = 2 * WIDTH_A + 3 * WIDTH_B + 2 * D_MODEL

LANES = 128
ROW_TILE = 256
FF_TILE = 256
V_EXT_ROWS = HEAD_DIM_B + 16
VMEM_LIMIT = 56 * 1024 * 1024
NEG_INF = float("-inf")
POS_INF = float("inf")
Q_SCALE = np.float32(HEAD_DIM_B ** -0.5 * np.log2(np.e))


def _gelu(x):
    return 0.5 * x * (1.0 + lax.erf(x * np.float32(np.sqrt(0.5))))


def _layer_norm(x, g, b):
    mu = jnp.mean(x, axis=-1, keepdims=True)
    xc = x - mu
    var = jnp.mean(xc * xc, axis=-1, keepdims=True)
    return xc * lax.rsqrt(var + LN_EPS) * g + b


def _dot(a, b):
    return jnp.dot(a, b, preferred_element_type=F32)


def _dot_nt(a, b):
    return lax.dot_general(a, b, (((1,), (1,)), ((), ())), preferred_element_type=F32)


def _dot_tn(a, b):
    return lax.dot_general(a, b, (((0,), (0,)), ((), ())), preferred_element_type=F32)


def _rope(t, cos, sin_signed):
    lane = lax.broadcasted_iota(jnp.int32, (t.shape[0], LANES), 1)
    first_half = (lane % HEAD_DIM_B) < (HEAD_DIM_B // 2)
    out = []
    for c in range(t.shape[1] // LANES):
        blk = t[:, c * LANES:(c + 1) * LANES]
        up = pltpu.roll(blk, LANES - HEAD_DIM_B // 2, 1)
        down = pltpu.roll(blk, HEAD_DIM_B // 2, 1)
        partner = jnp.where(first_half, up, down)
        out.append(blk * cos + partner * sin_signed)
    return jnp.concatenate(out, axis=1)


def _proj_kernel(x_ref, w_ref, wvt_ref, cos_ref, sin_ref, lng_ref, lnb_ref, wsp_ref, bsp_ref,
                 ya_ref, q_ref, k_ref, vext_ref, g_ref):
    xb = x_ref[...].astype(BF16)
    gu = _gelu(_dot(xb, w_ref[:, 0:WIDTH_A]))
    gv = _gelu(_dot(xb, w_ref[:, WIDTH_A:2 * WIDTH_A]))
    vn = _layer_norm(gv, lng_ref[...], lnb_ref[...]).astype(BF16)

    t_idx = lax.broadcasted_iota(jnp.int32, (CHUNK, 2 * CHUNK), 0)
    s_idx = lax.broadcasted_iota(jnp.int32, (CHUNK, 2 * CHUNK), 1) % CHUNK
    causal = s_idx <= t_idx
    lane = lax.broadcasted_iota(jnp.int32, (CHUNK, LANES), 1)
    left = lane < GROUP_DIM_A
    zero = jnp.zeros((CHUNK, LANES), BF16)
    for c in range(ROW_TILE // CHUNK):
        rows = slice(c * CHUNK, (c + 1) * CHUNK)
        mixed = []
        for p in range(N_GROUPS_A // 2):
            blk = vn[rows, p * LANES:(p + 1) * LANES]
            stacked = jnp.concatenate([jnp.where(left, blk, zero), jnp.where(left, zero, blk)], axis=0)
            w_pair = jnp.where(causal, wsp_ref[p], jnp.zeros((), BF16))
            mixed.append(_dot(w_pair, stacked))
        mixed = jnp.concatenate(mixed, axis=1) + bsp_ref[...]
        ya_ref[rows, :] = (gu[rows, :] * mixed).astype(BF16)

    cos = cos_ref[...]
    sin = sin_ref[...]
    q = _rope(_dot(xb, w_ref[:, 2 * WIDTH_A:2 * WIDTH_A + WIDTH_B]), cos, sin)
    q_ref[0] = (q * Q_SCALE).astype(BF16)
    k = _rope(_dot(xb, w_ref[:, 2 * WIDTH_A + WIDTH_B:2 * WIDTH_A + 2 * WIDTH_B]), cos, sin)
    k_ref[0] = k.astype(BF16)
    vt = _dot_nt(wvt_ref[...], xb).astype(BF16)
    ones = jnp.ones((V_EXT_ROWS - HEAD_DIM_B, ROW_TILE), BF16)
    for h in range(N_HEADS_B):
        vext_ref[0, 0, h * V_EXT_ROWS:h * V_EXT_ROWS + HEAD_DIM_B, :] = \
            vt[h * HEAD_DIM_B:(h + 1) * HEAD_DIM_B, :]
        vext_ref[0, 0, h * V_EXT_ROWS + HEAD_DIM_B:(h + 1) * V_EXT_ROWS, :] = ones
    g_ref[...] = _dot(xb, w_ref[:, 2 * WIDTH_A + 3 * WIDTH_B:])


def _projection(x2, w_in_b, w_vt, cos_t, sin_t, ln_g, ln_b, w_sp, b_sp, bsz, seq):
    n_rows = x2.shape[0]
    tiles_per_seq = seq // ROW_TILE
    const = lambda i: (0, 0)
    seq_tile = lambda i: (i // tiles_per_seq, i % tiles_per_seq, 0)
    return pl.pallas_call(
        _proj_kernel,
        grid=(n_rows // ROW_TILE,),
        in_specs=[
            pl.BlockSpec((ROW_TILE, D_MODEL), lambda i: (i, 0)),
            pl.BlockSpec((D_MODEL, PROJ_WIDTH), const),
            pl.BlockSpec((WIDTH_B, D_MODEL), const),
            pl.BlockSpec((ROW_TILE, LANES), lambda i: (i % tiles_per_seq, 0)),
            pl.BlockSpec((ROW_TILE, LANES), lambda i: (i % tiles_per_seq, 0)),
            pl.BlockSpec((1, WIDTH_A), const),
            pl.BlockSpec((1, WIDTH_A), const),
            pl.BlockSpec((N_GROUPS_A // 2, CHUNK, 2 * CHUNK), lambda i: (0, 0, 0)),
            pl.BlockSpec((CHUNK, WIDTH_A), const),
        ],
        out_specs=[
            pl.BlockSpec((ROW_TILE, WIDTH_A), lambda i: (i, 0)),
            pl.BlockSpec((1, ROW_TILE, WIDTH_B), seq_tile),
            pl.BlockSpec((1, ROW_TILE, WIDTH_B), seq_tile),
            pl.BlockSpec((1, 1, N_HEADS_B * V_EXT_ROWS, ROW_TILE),
                         lambda i: (i // tiles_per_seq, i % tiles_per_seq, 0, 0)),
            pl.BlockSpec((ROW_TILE, 2 * D_MODEL), lambda i: (i, 0)),
        ],
        out_shape=[
            jax.ShapeDtypeStruct((n_rows, WIDTH_A), BF16),
            jax.ShapeDtypeStruct((bsz, seq, WIDTH_B), BF16),
            jax.ShapeDtypeStruct((bsz, seq, WIDTH_B), BF16),
            jax.ShapeDtypeStruct((bsz, tiles_per_seq, N_HEADS_B * V_EXT_ROWS, ROW_TILE), BF16),
            jax.ShapeDtypeStruct((n_rows, 2 * D_MODEL), F32),
        ],
        compiler_params=pltpu.CompilerParams(
            dimension_semantics=("arbitrary",), vmem_limit_bytes=VMEM_LIMIT),
        name="sgu_moba_projection",
    )(x2, w_in_b, w_vt, cos_t, sin_t, ln_g, ln_b, w_sp, b_sp)


def _attn_kernel(q_ref, k_ref, vext_ref, o_ref, kmean_ref, qm_ref, pen_ref, m_ref, acc_ref, *, n_blocks):
    i = pl.program_id(1)
    pair_heads = LANES // HEAD_DIM_B

    @pl.when(i == 0)
    def _():
        k_sum = jnp.sum(k_ref[0].astype(F32).reshape(n_blocks, MOBA_BLOCK, WIDTH_B), axis=1)
        kmean_ref[...] = k_sum * np.float32(1.0 / MOBA_BLOCK)

    row0 = pl.multiple_of(i * MOBA_BLOCK, MOBA_BLOCK)
    lane_q = lax.broadcasted_iota(jnp.int32, (MOBA_BLOCK, LANES), 1)
    lane_k = lax.broadcasted_iota(jnp.int32, (n_blocks, LANES), 1)
    blk_row = lax.broadcasted_iota(jnp.int32, (n_blocks, MOBA_BLOCK), 0)
    key_pos = lax.broadcasted_iota(jnp.int32, (MOBA_BLOCK, MOBA_BLOCK), 0)
    qry_pos = lax.broadcasted_iota(jnp.int32, (MOBA_BLOCK, MOBA_BLOCK), 1)
    past = blk_row < i

    n_pairs = N_HEADS_B // pair_heads
    q_pairs = [q_ref[0, :, p * LANES:(p + 1) * LANES] for p in range(n_pairs)]
    k_own = [k_ref[0, pl.ds(row0, MOBA_BLOCK), p * LANES:(p + 1) * LANES] for p in range(n_pairs)]
    km_pairs = [kmean_ref[:, p * LANES:(p + 1) * LANES] for p in range(n_pairs)]
    v_own = [vext_ref[0, i, h * V_EXT_ROWS:(h + 1) * V_EXT_ROWS, :] for h in range(N_HEADS_B)]
    qm_all, s_all, gate_all = [], [], []
    for h in range(N_HEADS_B):
        in_head = lane_q // HEAD_DIM_B == h % pair_heads
        qm = jnp.where(in_head, q_pairs[h // pair_heads], jnp.zeros((), BF16))
        qm_all.append(qm)
        s_all.append(_dot_nt(k_own[h // pair_heads], qm))
    for h in range(N_HEADS_B):
        km = jnp.where(lane_k // HEAD_DIM_B == h % pair_heads, km_pairs[h // pair_heads], 0.0)
        km_hi = km.astype(BF16)
        km_lo = (km - km_hi.astype(F32)).astype(BF16)
        gate_all.append(_dot_nt(km_hi, qm_all[h]) + _dot_nt(km_lo, qm_all[h]))

    pen_all, m_all, acc_all = [], [], []
    for h in range(N_HEADS_B):
        s = jnp.where(key_pos <= qry_pos, s_all[h], NEG_INF)
        m0 = jnp.max(s, axis=0, keepdims=True)
        m_all.append(m0)
        acc_all.append(_dot(v_own[h], jnp.exp2(s - m0).astype(BF16)))
    for h in range(N_HEADS_B):
        gate = jnp.where(past, gate_all[h], NEG_INF)
        sel = jnp.zeros(gate.shape, jnp.bool_)
        for _ in range(MOBA_TOPK):
            best = jnp.max(gate, axis=0, keepdims=True)
            first = jnp.min(jnp.where(gate == best, blk_row, n_blocks), axis=0, keepdims=True)
            pick = blk_row == first
            sel = sel | pick
            gate = jnp.where(pick, NEG_INF, gate)
        pen_all.append(jnp.where(sel & past, 0.0, POS_INF).astype(F32))
    for h in range(N_HEADS_B):
        qm_ref[h] = qm_all[h]
        pen_ref[h] = pen_all[h]
        m_ref[h] = m_all[h]
        acc_ref[h] = acc_all[h]

    def kv_block(j, carry):
        col0 = pl.multiple_of(j * MOBA_BLOCK, MOBA_BLOCK)
        k_j = [k_ref[0, pl.ds(col0, MOBA_BLOCK), p * LANES:(p + 1) * LANES]
               for p in range(N_HEADS_B // pair_heads)]
        s_all = [_dot_nt(k_j[h // pair_heads], qm_ref[h]) for h in range(N_HEADS_B)]
        v_all = [vext_ref[0, j, h * V_EXT_ROWS:(h + 1) * V_EXT_ROWS, :] for h in range(N_HEADS_B)]
        pen_all = [pen_ref[h, pl.ds(j, 1), :] for h in range(N_HEADS_B)]
        m_all = [m_ref[h] for h in range(N_HEADS_B)]
        acc_all = [acc_ref[h] for h in range(N_HEADS_B)]
        m_out, acc_out = [], []
        for h in range(N_HEADS_B):
            m_new = jnp.maximum(m_all[h], jnp.max(s_all[h], axis=0, keepdims=True) - pen_all[h])
            p_j = jnp.exp2(s_all[h] - (m_new + pen_all[h])).astype(BF16)
            m_out.append(m_new)
            acc_out.append(jnp.exp2(m_all[h] - m_new) * acc_all[h] + _dot(v_all[h], p_j))
        for h in range(N_HEADS_B):
            m_ref[h] = m_out[h]
            acc_ref[h] = acc_out[h]
        return carry

    lax.fori_loop(0, i, kv_block, 0)

    for h in range(N_HEADS_B):
        acc = acc_ref[h]
        out = acc[0:HEAD_DIM_B, :] / acc[HEAD_DIM_B:HEAD_DIM_B + 1, :]
        o_ref[0, 0, h * HEAD_DIM_B:(h + 1) * HEAD_DIM_B, :] = out.astype(BF16)


def _moba_attention(q, k, vext, bsz, seq):
    n_blocks = seq // MOBA_BLOCK
    return pl.pallas_call(
        functools.partial(_attn_kernel, n_blocks=n_blocks),
        grid=(bsz, n_blocks),
        in_specs=[
            pl.BlockSpec((1, MOBA_BLOCK, WIDTH_B), lambda b, i: (b, i, 0)),
            pl.BlockSpec((1, seq, WIDTH_B), lambda b, i: (b, 0, 0)),
            pl.BlockSpec((1, n_blocks, N_HEADS_B * V_EXT_ROWS, MOBA_BLOCK), lambda b, i: (b, 0, 0, 0)),
        ],
        out_specs=pl.BlockSpec((1, 1, WIDTH_B, MOBA_BLOCK), lambda b, i: (b, i, 0, 0)),
        out_shape=jax.ShapeDtypeStruct((bsz, n_blocks, WIDTH_B, MOBA_BLOCK), BF16),
        scratch_shapes=[
            pltpu.VMEM((n_blocks, WIDTH_B), F32),
            pltpu.VMEM((N_HEADS_B, MOBA_BLOCK, LANES), BF16),
            pltpu.VMEM((N_HEADS_B, n_blocks, MOBA_BLOCK), F32),
            pltpu.VMEM((N_HEADS_B, 1, MOBA_BLOCK), F32),
            pltpu.VMEM((N_HEADS_B, V_EXT_ROWS, MOBA_BLOCK), F32),
        ],
        compiler_params=pltpu.CompilerParams(
            dimension_semantics=("arbitrary", "arbitrary"), vmem_limit_bytes=VMEM_LIMIT),
        name="moba_attention",
    )(q, k, vext)


def _merge_kernel(x_ref, ya_ref, ybt_ref, g_ref, bg_ref, wa_ref, wb_ref, wo_ref, lng_ref, lnb_ref,
                  o_ref, *, alpha):
    branch_a = _dot(ya_ref[...], wa_ref[...])
    branch_b = _dot_tn(ybt_ref[0, 0], wb_ref[...])
    g = g_ref[...] + bg_ref[...]
    merged = (jax.nn.sigmoid(g[:, :D_MODEL]) * branch_a
              + jax.nn.sigmoid(g[:, D_MODEL:]) * branch_b)
    mix = _dot(merged.astype(BF16), wo_ref[...])
    o_ref[...] = _layer_norm(alpha * x_ref[...] + mix, lng_ref[...], lnb_ref[...])


def _merge(x2, ya, ybt, g, b_gate, w_a, w_b, w_o, ln_g, ln_b, alpha, seq):
    n_rows = x2.shape[0]
    tiles_per_seq = seq // ROW_TILE
    const = lambda i: (0, 0)
    return pl.pallas_call(
        functools.partial(_merge_kernel, alpha=alpha),
        grid=(n_rows // ROW_TILE,),
        in_specs=[
            pl.BlockSpec((ROW_TILE, D_MODEL), lambda i: (i, 0)),
            pl.BlockSpec((ROW_TILE, WIDTH_A), lambda i: (i, 0)),
            pl.BlockSpec((1, 1, WIDTH_B, MOBA_BLOCK),
                         lambda i: (i // tiles_per_seq, i % tiles_per_seq, 0, 0)),
            pl.BlockSpec((ROW_TILE, 2 * D_MODEL), lambda i: (i, 0)),
            pl.BlockSpec((1, 2 * D_MODEL), const),
            pl.BlockSpec((WIDTH_A, D_MODEL), const),
            pl.BlockSpec((WIDTH_B, D_MODEL), const),
            pl.BlockSpec((D_MODEL, D_MODEL), const),
            pl.BlockSpec((1, D_MODEL), const),
            pl.BlockSpec((1, D_MODEL), const),
        ],
        out_specs=pl.BlockSpec((ROW_TILE, D_MODEL), lambda i: (i, 0)),
        out_shape=jax.ShapeDtypeStruct((n_rows, D_MODEL), F32),
        compiler_params=pltpu.CompilerParams(
            dimension_semantics=("arbitrary",), vmem_limit_bytes=VMEM_LIMIT),
        name="merge_outproj_ln",
    )(x2, ya, ybt, g, b_gate, w_a, w_b, w_o, ln_g, ln_b)


def _ffn_kernel(x_ref, wup_ref, cw_ref, cb_ref, wdn_ref, lng_ref, lnb_ref, o_ref,
                hbuf_ref, carry_ref, acc_ref, *, alpha, tiles_per_seq):
    i = pl.program_id(0)

    @pl.when(i % tiles_per_seq == 0)
    def _():
        carry_ref[...] = jnp.zeros(carry_ref.shape, F32)

    x = x_ref[...]
    xb = x.astype(BF16)
    acc_ref[...] = jnp.zeros(acc_ref.shape, F32)
    pad = carry_ref.shape[0]
    for j in range(D_FF // FF_TILE):
        conv = []
        for half in range(2):
            c0 = half * D_FF + j * FF_TILE
            cols = slice(c0, c0 + FF_TILE)
            hbuf_ref[0:pad, :] = carry_ref[:, cols]
            h = _dot(xb, wup_ref[:, cols])
            hbuf_ref[pad:, :] = h
            carry_ref[:, cols] = h[ROW_TILE - pad:, :]
            cw = cw_ref[:, cols]
            conv.append(cw[0:1, :] * hbuf_ref[pad - 2:pad - 2 + ROW_TILE, :]
                        + cw[1:2, :] * hbuf_ref[pad - 1:pad - 1 + ROW_TILE, :]
                        + cw[2:3, :] * h + cb_ref[:, cols])
        act = (_gelu(conv[0]) * conv[1]).astype(BF16)
        acc_ref[...] += _dot(act, wdn_ref[j * FF_TILE:(j + 1) * FF_TILE, :])
    o_ref[...] = _layer_norm(alpha * x + acc_ref[...], lng_ref[...], lnb_ref[...])


def _conv_ffn(x1, w_up, conv_w, conv_b, w_dn, ln_g, ln_b, alpha, seq):
    n_rows = x1.shape[0]
    tiles_per_seq = seq // ROW_TILE
    const = lambda i: (0, 0)
    pad = 8
    return pl.pallas_call(
        functools.partial(_ffn_kernel, alpha=alpha, tiles_per_seq=tiles_per_seq),
        grid=(n_rows // ROW_TILE,),
        in_specs=[
            pl.BlockSpec((ROW_TILE, D_MODEL), lambda i: (i, 0)),
            pl.BlockSpec((D_MODEL, 2 * D_FF), const),
            pl.BlockSpec((CONV_WIDTH, 2 * D_FF), const),
            pl.BlockSpec((1, 2 * D_FF), const),
            pl.BlockSpec((D_FF, D_MODEL), const),
            pl.BlockSpec((1, D_MODEL), const),
            pl.BlockSpec((1, D_MODEL), const),
        ],
        out_specs=pl.BlockSpec((ROW_TILE, D_MODEL), lambda i: (i, 0)),
        out_shape=jax.ShapeDtypeStruct((n_rows, D_MODEL), F32),
        scratch_shapes=[
            pltpu.VMEM((pad + ROW_TILE, FF_TILE), F32),
            pltpu.VMEM((pad, 2 * D_FF), F32),
            pltpu.VMEM((ROW_TILE, D_MODEL), F32),
        ],
        compiler_params=pltpu.CompilerParams(
            dimension_semantics=("arbitrary",), vmem_limit_bytes=VMEM_LIMIT),
        name="conv_ffn_ln",
    )(x1, w_up, conv_w, conv_b, w_dn, ln_g, ln_b)


def _rope_tables(seq):
    half = HEAD_DIM_B // 2
    inv_freq = ROPE_THETA ** (-jnp.arange(half, dtype=F32) / half)
    ang = jnp.arange(seq, dtype=F32)[:, None] * inv_freq[None, :]
    cos = jnp.tile(jnp.cos(ang), (1, LANES // half))
    sin = jnp.tile(jnp.sin(ang), (1, LANES // half))
    first_half = (jnp.arange(LANES) % HEAD_DIM_B) < half
    return cos, jnp.where(first_half[None, :], -sin, sin)


def kernel(x, w_in, b_gate, sgu_ln_g, sgu_ln_b, w_spatial, b_spatial, w_branch_a, w_branch_b,
           w_out, ln1_g, ln1_b, w_up, conv_w, conv_b, w_down, ln2_g, ln2_b):
    bsz, seq, d_model = x.shape
    depth = w_in.shape[0]
    assert d_model == D_MODEL and seq % ROW_TILE == 0 and ROW_TILE == MOBA_BLOCK
    alpha = float((2.0 * depth) ** 0.25)
    cos_t, sin_t = _rope_tables(seq)
    x2 = x.reshape(bsz * seq, d_model)
    for l in range(depth):
        w_in_b = w_in[l].astype(BF16)
        w_vt = w_in_b[:, 2 * WIDTH_A + 2 * WIDTH_B:2 * WIDTH_A + 3 * WIDTH_B].T
        w_sp = w_spatial[l].astype(BF16).reshape(N_GROUPS_A // 2, 2, CHUNK, CHUNK)
        w_sp = w_sp.transpose(0, 2, 1, 3).reshape(N_GROUPS_A // 2, CHUNK, 2 * CHUNK)
        b_sp = jnp.repeat(b_spatial[l].T, GROUP_DIM_A, axis=1)
        ya, q, k, vext, g = _projection(
            x2, w_in_b, w_vt, cos_t, sin_t, sgu_ln_g[l][None, :], sgu_ln_b[l][None, :],
            w_sp, b_sp, bsz, seq)
        ybt = _moba_attention(q, k, vext, bsz, seq)
        x1 = _merge(x2, ya, ybt, g, b_gate[l][None, :], w_branch_a[l].astype(BF16),
                    w_branch_b[l].astype(BF16), w_out[l].astype(BF16),
                    ln1_g[l][None, :], ln1_b[l][None, :], alpha, seq)
        x2 = _conv_ffn(x1, w_up[l].astype(BF16), conv_w[l], conv_b[l][None, :],
                       w_down[l].astype(BF16), ln2_g[l][None, :], ln2_b[l][None, :], alpha, seq)
    return x2.reshape(bsz, seq, d_model)
```

```python
import functools

import jax
import jax.numpy as jnp
import numpy as np
from jax import lax
from jax.experimental import pallas as pl
from jax.experimental.pallas import tpu as pltpu

F32 = jnp.float32
BF16 = jnp.bfloat16

D_MODEL = 1024
N_GROUPS_A = 8
GROUP_DIM_A = 64
WIDTH_A = N_GROUPS_A * GROUP_DIM_A
CHUNK = 128
N_HEADS_B = 8
HEAD_DIM_B = 64
WIDTH_B = N_HEADS_B * HEAD_DIM_B
MOBA_BLOCK = 256
MOBA_TOPK = 3
ROPE_THETA = 10000.0
D_FF = 2816
CONV_WIDTH = 3
LN_EPS = 1e-5
PROJ_WIDTH = 2 * WIDTH_A + 3 * WIDTH_B + 2 * D_MODEL

LANES = 128
PROJ_ROWS = 512
MERGE_ROWS = 512
FF_TILE = 256
FFN_ROWS = 512
UP_LOOKAHEAD = 3
V_EXT_ROWS = HEAD_DIM_B + 16
VMEM_LIMIT = 56 * 1024 * 1024
NEG_INF = float("-inf")
POS_INF = float("inf")
Q_SCALE = np.float32(HEAD_DIM_B ** -0.5 * np.log2(np.e))


def _gelu(x):
    return 0.5 * x * (1.0 + lax.erf(x * np.float32(np.sqrt(0.5))))


def _layer_norm(x, g, b):
    mu = jnp.mean(x, axis=-1, keepdims=True)
    xc = x - mu
    var = jnp.mean(xc * xc, axis=-1, keepdims=True)
    return xc * lax.rsqrt(var + LN_EPS) * g + b


def _dot(a, b):
    return jnp.dot(a, b, preferred_element_type=F32)


def _dot_nt(a, b):
    return lax.dot_general(a, b, (((1,), (1,)), ((), ())), preferred_element_type=F32)


def _dot_tn(a, b):
    return lax.dot_general(a, b, (((0,), (0,)), ((), ())), preferred_element_type=F32)


def _rope(t, cos, sin_signed):
    lane = lax.broadcasted_iota(jnp.int32, (t.shape[0], LANES), 1)
    first_half = (lane % HEAD_DIM_B) < (HEAD_DIM_B // 2)
    out = []
    for c in range(t.shape[1] // LANES):
        blk = t[:, c * LANES:(c + 1) * LANES]
        up = pltpu.roll(blk, LANES - HEAD_DIM_B // 2, 1)
        down = pltpu.roll(blk, HEAD_DIM_B // 2, 1)
        partner = jnp.where(first_half, up, down)
        out.append(blk * cos + partner * sin_signed)
    return jnp.concatenate(out, axis=1)


def _proj_kernel(x_ref, w_ref, wvt_ref, cos_ref, sin_ref, lng_ref, lnb_ref, wsp_ref, bsp_ref,
                 ya_ref, q_ref, k_ref, vext_ref, g_ref):
    xb = x_ref[...].astype(BF16)
    gu = _gelu(_dot(xb, w_ref[:, 0:WIDTH_A]))
    gv = _gelu(_dot(xb, w_ref[:, WIDTH_A:2 * WIDTH_A]))
    g_ref[...] = _dot(xb, w_ref[:, 2 * WIDTH_A + 3 * WIDTH_B:]).astype(BF16)
    cos = cos_ref[...]
    sin = sin_ref[...]
    q = _rope(_dot(xb, w_ref[:, 2 * WIDTH_A:2 * WIDTH_A + WIDTH_B]), cos, sin)
    q_ref[0] = (q * Q_SCALE).astype(BF16)
    k = _rope(_dot(xb, w_ref[:, 2 * WIDTH_A + WIDTH_B:2 * WIDTH_A + 2 * WIDTH_B]), cos, sin)
    k_ref[0] = k.astype(BF16)
    vt = _dot_nt(wvt_ref[...], xb).astype(BF16)
    ones = jnp.ones((V_EXT_ROWS - HEAD_DIM_B, MOBA_BLOCK), BF16)
    for t in range(PROJ_ROWS // MOBA_BLOCK):
        for h in range(N_HEADS_B):
            vext_ref[0, t, h * V_EXT_ROWS:h * V_EXT_ROWS + HEAD_DIM_B, :] = \
                vt[h * HEAD_DIM_B:(h + 1) * HEAD_DIM_B, t * MOBA_BLOCK:(t + 1) * MOBA_BLOCK]
            vext_ref[0, t, h * V_EXT_ROWS + HEAD_DIM_B:(h + 1) * V_EXT_ROWS, :] = ones

    vn = _layer_norm(gv, lng_ref[...], lnb_ref[...]).astype(BF16)
    t_idx = lax.broadcasted_iota(jnp.int32, (CHUNK, 2 * CHUNK), 0)
    s_idx = lax.broadcasted_iota(jnp.int32, (CHUNK, 2 * CHUNK), 1) % CHUNK
    causal = s_idx <= t_idx
    w_pairs = [jnp.where(causal, wsp_ref[p], jnp.zeros((), BF16)) for p in range(N_GROUPS_A // 2)]
    lane = lax.broadcasted_iota(jnp.int32, (CHUNK, LANES), 1)
    left = lane < GROUP_DIM_A
    zero = jnp.zeros((CHUNK, LANES), BF16)
    for c in range(PROJ_ROWS // CHUNK):
        rows = slice(c * CHUNK, (c + 1) * CHUNK)
        mixed = []
        for p in range(N_GROUPS_A // 2):
            blk = vn[rows, p * LANES:(p + 1) * LANES]
            stacked = jnp.concatenate([jnp.where(left, blk, zero), jnp.where(left, zero, blk)], axis=0)
            mixed.append(_dot(w_pairs[p], stacked))
        mixed = jnp.concatenate(mixed, axis=1) + bsp_ref[...]
        ya_ref[rows, :] = (gu[rows, :] * mixed).astype(BF16)


def _projection(x2, w_in_b, w_vt, cos_t, sin_t, ln_g, ln_b, w_sp, b_sp, bsz, seq):
    n_rows = x2.shape[0]
    tiles_per_seq = seq // PROJ_ROWS
    blocks_per_tile = PROJ_ROWS // MOBA_BLOCK
    const = lambda i: (0, 0)
    resident = pl.Buffered(1)
    seq_tile = lambda i: (i // tiles_per_seq, i % tiles_per_seq, 0)
    return pl.pallas_call(
        _proj_kernel,
        grid=(n_rows // PROJ_ROWS,),
        in_specs=[
            pl.BlockSpec((PROJ_ROWS, D_MODEL), lambda i: (i, 0)),
            pl.BlockSpec((D_MODEL, PROJ_WIDTH), const, pipeline_mode=resident),
            pl.BlockSpec((WIDTH_B, D_MODEL), const, pipeline_mode=resident),
            pl.BlockSpec((PROJ_ROWS, LANES), lambda i: (i % tiles_per_seq, 0)),
            pl.BlockSpec((PROJ_ROWS, LANES), lambda i: (i % tiles_per_seq, 0)),
            pl.BlockSpec((1, WIDTH_A), const, pipeline_mode=resident),
            pl.BlockSpec((1, WIDTH_A), const, pipeline_mode=resident),
            pl.BlockSpec((N_GROUPS_A // 2, CHUNK, 2 * CHUNK), lambda i: (0, 0, 0),
                         pipeline_mode=resident),
            pl.BlockSpec((CHUNK, WIDTH_A), const, pipeline_mode=resident),
        ],
        out_specs=[
            pl.BlockSpec((PROJ_ROWS, WIDTH_A), lambda i: (i, 0)),
            pl.BlockSpec((1, PROJ_ROWS, WIDTH_B), seq_tile),
            pl.BlockSpec((1, PROJ_ROWS, WIDTH_B), seq_tile),
            pl.BlockSpec((1, blocks_per_tile, N_HEADS_B * V_EXT_ROWS, MOBA_BLOCK),
                         lambda i: (i // tiles_per_seq, i % tiles_per_seq, 0, 0)),
            pl.BlockSpec((PROJ_ROWS, 2 * D_MODEL), lambda i: (i, 0)),
        ],
        out_shape=[
            jax.ShapeDtypeStruct((n_rows, WIDTH_A), BF16),
            jax.ShapeDtypeStruct((bsz, seq, WIDTH_B), BF16),
            jax.ShapeDtypeStruct((bsz, seq, WIDTH_B), BF16),
            jax.ShapeDtypeStruct((bsz, seq // MOBA_BLOCK, N_HEADS_B * V_EXT_ROWS, MOBA_BLOCK), BF16),
            jax.ShapeDtypeStruct((n_rows, 2 * D_MODEL), BF16),
        ],
        compiler_params=pltpu.CompilerParams(
            dimension_semantics=("arbitrary",), vmem_limit_bytes=VMEM_LIMIT),
        name="sgu_moba_projection",
    )(x2, w_in_b, w_vt, cos_t, sin_t, ln_g, ln_b, w_sp, b_sp)


def _attn_kernel(q_ref, k_ref, vext_ref, o_ref, kmean_ref, qm_ref, pen_ref, m_ref, acc_ref,
                 sa_ref, samax_ref, sb_ref, sbmax_ref, *, n_blocks):
    i = pl.program_id(1)
    pair_heads = LANES // HEAD_DIM_B

    @pl.when(i == 0)
    def _():
        k_sum = jnp.sum(k_ref[0].astype(F32).reshape(n_blocks, MOBA_BLOCK, WIDTH_B), axis=1)
        kmean_ref[...] = k_sum * np.float32(1.0 / MOBA_BLOCK)

    row0 = pl.multiple_of(i * MOBA_BLOCK, MOBA_BLOCK)
    lane_q = lax.broadcasted_iota(jnp.int32, (MOBA_BLOCK, LANES), 1)
    lane_k = lax.broadcasted_iota(jnp.int32, (n_blocks, LANES), 1)
    blk_row = lax.broadcasted_iota(jnp.int32, (n_blocks, MOBA_BLOCK), 0)
    key_pos = lax.broadcasted_iota(jnp.int32, (MOBA_BLOCK, MOBA_BLOCK), 0)
    qry_pos = lax.broadcasted_iota(jnp.int32, (MOBA_BLOCK, MOBA_BLOCK), 1)
    past = blk_row < i

    n_pairs = N_HEADS_B // pair_heads
    q_pairs = [q_ref[0, :, p * LANES:(p + 1) * LANES] for p in range(n_pairs)]
    k_own = [k_ref[0, pl.ds(row0, MOBA_BLOCK), p * LANES:(p + 1) * LANES] for p in range(n_pairs)]
    km_pairs = [kmean_ref[:, p * LANES:(p + 1) * LANES] for p in range(n_pairs)]
    v_own = [vext_ref[0, i, h * V_EXT_ROWS:(h + 1) * V_EXT_ROWS, :] for h in range(N_HEADS_B)]
    qm_all, s_all, gate_all = [], [], []
    for h in range(N_HEADS_B):
        in_head = lane_q // HEAD_DIM_B == h % pair_heads
        qm = jnp.where(in_head, q_pairs[h // pair_heads], jnp.zeros((), BF16))
        qm_all.append(qm)
        s_all.append(_dot_nt(k_own[h // pair_heads], qm))
    for h in range(N_HEADS_B):
        km = jnp.where(lane_k // HEAD_DIM_B == h % pair_heads, km_pairs[h // pair_heads], 0.0)
        km_hi = km.astype(BF16)
        km_lo = (km - km_hi.astype(F32)).astype(BF16)
        gate_all.append(_dot_nt(km_hi, qm_all[h]) + _dot_nt(km_lo, qm_all[h]))

    pen_all, m_all, acc_all = [], [], []
    for h in range(N_HEADS_B):
        s = jnp.where(key_pos <= qry_pos, s_all[h], NEG_INF)
        m0 = jnp.max(s, axis=0, keepdims=True)
        m_all.append(m0)
        acc_all.append(_dot(v_own[h], jnp.exp2(s - m0).astype(BF16)))
    for h in range(N_HEADS_B):
        gate = jnp.where(past, gate_all[h], NEG_INF)
        sel = jnp.zeros(gate.shape, jnp.bool_)
        for _ in range(MOBA_TOPK):
            best = jnp.max(gate, axis=0, keepdims=True)
            first = jnp.min(jnp.where(gate == best, blk_row, n_blocks), axis=0, keepdims=True)
            pick = blk_row == first
            sel = sel | pick
            gate = jnp.where(pick, NEG_INF, gate)
        pen_all.append(jnp.where(sel & past, 0.0, POS_INF).astype(F32))
    for h in range(N_HEADS_B):
        qm_ref[h] = qm_all[h]
        pen_ref[h] = pen_all[h]
        m_ref[h] = m_all[h]
        acc_ref[h] = acc_all[h]

    def load_keys(j):
        col0 = pl.multiple_of(j * MOBA_BLOCK, MOBA_BLOCK)
        return [k_ref[0, pl.ds(col0, MOBA_BLOCK), p * LANES:(p + 1) * LANES] for p in range(n_pairs)]

    def load_block(j):
        v_all = [vext_ref[0, j, h * V_EXT_ROWS:(h + 1) * V_EXT_ROWS, :] for h in range(N_HEADS_B)]
        pen_all = [pen_ref[h, pl.ds(j, 1), :] for h in range(N_HEADS_B)]
        return v_all, pen_all

    def scores(k_j, s_ref, smax_ref):
        s_all = [_dot_nt(k_j[h // pair_heads], qm_ref[h]) for h in range(N_HEADS_B)]
        for h in range(N_HEADS_B):
            s_ref[h] = s_all[h]
            smax_ref[h] = jnp.max(s_all[h], axis=0, keepdims=True)

    def absorb(block, s_ref, smax_ref):
        v_all, pen_all = block
        m_all = [m_ref[h] for h in range(N_HEADS_B)]
        acc_all = [acc_ref[h] for h in range(N_HEADS_B)]
        m_out, acc_out = [], []
        for h in range(N_HEADS_B):
            m_new = jnp.maximum(m_all[h], smax_ref[h] - pen_all[h])
            p_j = jnp.exp2(s_ref[h] - (m_new + pen_all[h])).astype(BF16)
            m_out.append(m_new)
            acc_out.append(jnp.exp2(m_all[h] - m_new) * acc_all[h] + _dot(v_all[h], p_j))
        for h in range(N_HEADS_B):
            m_ref[h] = m_out[h]
            acc_ref[h] = acc_out[h]

    @pl.when(i > 0)
    def _():
        scores(load_keys(0), sa_ref, samax_ref)

        def kv_pair(jj, carry):
            j = 2 * jj
            keys_b, keys_a = load_keys(j + 1), load_keys(j + 2)
            block_a, block_b = load_block(j), load_block(j + 1)
            scores(keys_b, sb_ref, sbmax_ref)
            absorb(block_a, sa_ref, samax_ref)
            scores(keys_a, sa_ref, samax_ref)
            absorb(block_b, sb_ref, sbmax_ref)
            return carry

        lax.fori_loop(0, i // 2, kv_pair, 0)

    @pl.when(i % 2 == 1)
    def _():
        absorb(load_block(i - 1), sa_ref, samax_ref)

    for h in range(N_HEADS_B):
        acc = acc_ref[h]
        out = acc[0:HEAD_DIM_B, :] / acc[HEAD_DIM_B:HEAD_DIM_B + 1, :]
        o_ref[0, 0, h * HEAD_DIM_B:(h + 1) * HEAD_DIM_B, :] = out.astype(BF16)


def _moba_attention(q, k, vext, bsz, seq):
    n_blocks = seq // MOBA_BLOCK
    return pl.pallas_call(
        functools.partial(_attn_kernel, n_blocks=n_blocks),
        grid=(bsz, n_blocks),
        in_specs=[
            pl.BlockSpec((1, MOBA_BLOCK, WIDTH_B), lambda b, i: (b, i, 0)),
            pl.BlockSpec((1, seq, WIDTH_B), lambda b, i: (b, 0, 0)),
            pl.BlockSpec((1, n_blocks, N_HEADS_B * V_EXT_ROWS, MOBA_BLOCK), lambda b, i: (b, 0, 0, 0)),
        ],
        out_specs=pl.BlockSpec((1, 1, WIDTH_B, MOBA_BLOCK), lambda b, i: (b, i, 0, 0)),
        out_shape=jax.ShapeDtypeStruct((bsz, n_blocks, WIDTH_B, MOBA_BLOCK), BF16),
        scratch_shapes=[
            pltpu.VMEM((n_blocks, WIDTH_B), F32),
            pltpu.VMEM((N_HEADS_B, MOBA_BLOCK, LANES), BF16),
            pltpu.VMEM((N_HEADS_B, n_blocks, MOBA_BLOCK), F32),
            pltpu.VMEM((N_HEADS_B, 1, MOBA_BLOCK), F32),
            pltpu.VMEM((N_HEADS_B, V_EXT_ROWS, MOBA_BLOCK), F32),
            pltpu.VMEM((N_HEADS_B, MOBA_BLOCK, MOBA_BLOCK), F32),
            pltpu.VMEM((N_HEADS_B, 1, MOBA_BLOCK), F32),
            pltpu.VMEM((N_HEADS_B, MOBA_BLOCK, MOBA_BLOCK), F32),
            pltpu.VMEM((N_HEADS_B, 1, MOBA_BLOCK), F32),
        ],
        compiler_params=pltpu.CompilerParams(
            dimension_semantics=("arbitrary", "arbitrary"), vmem_limit_bytes=VMEM_LIMIT),
        name="moba_attention",
    )(q, k, vext)


def _merge_kernel(x_ref, ya_ref, ybt_ref, g_ref, bg_ref, wa_ref, wb_ref, wo_ref, lng_ref, lnb_ref,
                  o_ref, *, alpha):
    branch_a = _dot(ya_ref[...], wa_ref[...])
    branch_b = jnp.concatenate(
        [_dot_tn(ybt_ref[0, t], wb_ref[...]) for t in range(MERGE_ROWS // MOBA_BLOCK)], axis=0)
    g = g_ref[...].astype(F32) + bg_ref[...]
    merged = (jax.nn.sigmoid(g[:, :D_MODEL]) * branch_a
              + jax.nn.sigmoid(g[:, D_MODEL:]) * branch_b)
    mix = _dot(merged.astype(BF16), wo_ref[...])
    o_ref[...] = _layer_norm(alpha * x_ref[...] + mix, lng_ref[...], lnb_ref[...])


def _merge(x2, ya, ybt, g, b_gate, w_a, w_b, w_o, ln_g, ln_b, alpha, seq):
    n_rows = x2.shape[0]
    tiles_per_seq = seq // MERGE_ROWS
    const = lambda i: (0, 0)
    resident = pl.Buffered(1)
    return pl.pallas_call(
        functools.partial(_merge_kernel, alpha=alpha),
        grid=(n_rows // MERGE_ROWS,),
        in_specs=[
            pl.BlockSpec((MERGE_ROWS, D_MODEL), lambda i: (i, 0)),
            pl.BlockSpec((MERGE_ROWS, WIDTH_A), lambda i: (i, 0)),
            pl.BlockSpec((1, MERGE_ROWS // MOBA_BLOCK, WIDTH_B, MOBA_BLOCK),
                         lambda i: (i // tiles_per_seq, i % tiles_per_seq, 0, 0)),
            pl.BlockSpec((MERGE_ROWS, 2 * D_MODEL), lambda i: (i, 0)),
            pl.BlockSpec((1, 2 * D_MODEL), const, pipeline_mode=resident),
            pl.BlockSpec((WIDTH_A, D_MODEL), const, pipeline_mode=resident),
            pl.BlockSpec((WIDTH_B, D_MODEL), const, pipeline_mode=resident),
            pl.BlockSpec((D_MODEL, D_MODEL), const, pipeline_mode=resident),
            pl.BlockSpec((1, D_MODEL), const, pipeline_mode=resident),
            pl.BlockSpec((1, D_MODEL), const, pipeline_mode=resident),
        ],
        out_specs=pl.BlockSpec((MERGE_ROWS, D_MODEL), lambda i: (i, 0)),
        out_shape=jax.ShapeDtypeStruct((n_rows, D_MODEL), F32),
        compiler_params=pltpu.CompilerParams(
            dimension_semantics=("arbitrary",), vmem_limit_bytes=VMEM_LIMIT),
        name="merge_outproj_ln",
    )(x2, ya, ybt, g, b_gate, w_a, w_b, w_o, ln_g, ln_b)


def _ffn_kernel(x_ref, wup_ref, cw_ref, cb_ref, wdn_ref, lng_ref, lnb_ref, o_ref,
                hbuf_ref, carry_ref, acc_ref, *, alpha, tiles_per_seq):
    i = pl.program_id(0)

    @pl.when(i % tiles_per_seq == 0)
    def _():
        carry_ref[...] = jnp.zeros(carry_ref.shape, F32)

    x = x_ref[...]
    xb = x.astype(BF16)
    acc_ref[...] = jnp.zeros(acc_ref.shape, F32)
    pad = carry_ref.shape[0]
    n_steps = D_FF // FF_TILE

    def cols_of(j, half):
        c0 = half * D_FF + j * FF_TILE
        return slice(c0, c0 + FF_TILE)

    def up_proj(j):
        for half in range(2):
            cols = cols_of(j, half)
            hbuf = hbuf_ref.at[(2 * j + half) % hbuf_ref.shape[0]]
            hbuf[0:pad, :] = carry_ref[:, cols]
            h = _dot(xb, wup_ref[:, cols])
            hbuf[pad:, :] = h
            carry_ref[:, cols] = h[FFN_ROWS - pad:, :]

    def conv_act(j):
        conv = []
        for half in range(2):
            cols = cols_of(j, half)
            hbuf = hbuf_ref.at[(2 * j + half) % hbuf_ref.shape[0]]
            cw = cw_ref[:, cols]
            conv.append(cw[0:1, :] * hbuf[pad - 2:pad - 2 + FFN_ROWS, :]
                        + cw[1:2, :] * hbuf[pad - 1:pad - 1 + FFN_ROWS, :]
                        + cw[2:3, :] * hbuf[pad:, :] + cb_ref[:, cols])
        return (_gelu(conv[0]) * conv[1]).astype(BF16)

    for j in range(min(UP_LOOKAHEAD, n_steps)):
        up_proj(j)
    for j in range(n_steps):
        if j + UP_LOOKAHEAD < n_steps:
            up_proj(j + UP_LOOKAHEAD)
        acc_ref[...] += _dot(conv_act(j), wdn_ref[j * FF_TILE:(j + 1) * FF_TILE, :])
    o_ref[...] = _layer_norm(alpha * x + acc_ref[...], lng_ref[...], lnb_ref[...])


def _conv_ffn(x1, w_up, conv_w, conv_b, w_dn, ln_g, ln_b, alpha, seq):
    n_rows = x1.shape[0]
    tiles_per_seq = seq // FFN_ROWS
    const = lambda i: (0, 0)
    resident = pl.Buffered(1)
    pad = 8
    return pl.pallas_call(
        functools.partial(_ffn_kernel, alpha=alpha, tiles_per_seq=tiles_per_seq),
        grid=(n_rows // FFN_ROWS,),
        in_specs=[
            pl.BlockSpec((FFN_ROWS, D_MODEL), lambda i: (i, 0)),
            pl.BlockSpec((D_MODEL, 2 * D_FF), const, pipeline_mode=resident),
            pl.BlockSpec((CONV_WIDTH, 2 * D_FF), const, pipeline_mode=resident),
            pl.BlockSpec((1, 2 * D_FF), const, pipeline_mode=resident),
            pl.BlockSpec((D_FF, D_MODEL), const, pipeline_mode=resident),
            pl.BlockSpec((1, D_MODEL), const, pipeline_mode=resident),
            pl.BlockSpec((1, D_MODEL), const, pipeline_mode=resident),
        ],
        out_specs=pl.BlockSpec((FFN_ROWS, D_MODEL), lambda i: (i, 0)),
        out_shape=jax.ShapeDtypeStruct((n_rows, D_MODEL), F32),
        scratch_shapes=[
            pltpu.VMEM((2 * (UP_LOOKAHEAD + 1), pad + FFN_ROWS, FF_TILE), F32),
            pltpu.VMEM((pad, 2 * D_FF), F32),
            pltpu.VMEM((FFN_ROWS, D_MODEL), F32),
        ],
        compiler_params=pltpu.CompilerParams(
            dimension_semantics=("arbitrary",), vmem_limit_bytes=VMEM_LIMIT),
        name="conv_ffn_ln",
    )(x1, w_up, conv_w, conv_b, w_dn, ln_g, ln_b)


def _rope_tables(seq):
    half = HEAD_DIM_B // 2
    inv_freq = ROPE_THETA ** (-jnp.arange(half, dtype=F32) / half)
    ang = jnp.arange(seq, dtype=F32)[:, None] * inv_freq[None, :]
    cos = jnp.tile(jnp.cos(ang), (1, LANES // half))
    sin = jnp.tile(jnp.sin(ang), (1, LANES // half))
    first_half = (jnp.arange(LANES) % HEAD_DIM_B) < half
    return cos, jnp.where(first_half[None, :], -sin, sin)


def kernel(x, w_in, b_gate, sgu_ln_g, sgu_ln_b, w_spatial, b_spatial, w_branch_a, w_branch_b,
           w_out, ln1_g, ln1_b, w_up, conv_w, conv_b, w_down, ln2_g, ln2_b):
    bsz, seq, d_model = x.shape
    depth = w_in.shape[0]
    assert d_model == D_MODEL and all(seq % t == 0 for t in (PROJ_ROWS, MERGE_ROWS, FFN_ROWS))
    assert all(t % MOBA_BLOCK == 0 for t in (PROJ_ROWS, MERGE_ROWS))
    alpha = float((2.0 * depth) ** 0.25)
    cos_t, sin_t = _rope_tables(seq)
    x2 = x.reshape(bsz * seq, d_model)
    for l in range(depth):
        w_in_b = w_in[l].astype(BF16)
        w_vt = w_in_b[:, 2 * WIDTH_A + 2 * WIDTH_B:2 * WIDTH_A + 3 * WIDTH_B].T
        w_sp = w_spatial[l].astype(BF16).reshape(N_GROUPS_A // 2, 2, CHUNK, CHUNK)
        w_sp = w_sp.transpose(0, 2, 1, 3).reshape(N_GROUPS_A // 2, CHUNK, 2 * CHUNK)
        b_sp = jnp.repeat(b_spatial[l].T, GROUP_DIM_A, axis=1)
        ya, q, k, vext, g = _projection(
            x2, w_in_b, w_vt, cos_t, sin_t, sgu_ln_g[l][None, :], sgu_ln_b[l][None, :],
            w_sp, b_sp, bsz, seq)
        ybt = _moba_attention(q, k, vext, bsz, seq)
        x1 = _merge(x2, ya, ybt, g, b_gate[l][None, :], w_branch_a[l].astype(BF16),
                    w_branch_b[l].astype(BF16), w_out[l].astype(BF16),
                    ln1_g[l][None, :], ln1_b[l][None, :], alpha, seq)
        x2 = _conv_ffn(x1, w_up[l].astype(BF16), conv_w[l], conv_b[l][None, :],
                       w_down[l].astype(BF16), ln2_g[l][None, :], ln2_b[l][None, :], alpha, seq)
    return x2.reshape(bsz, seq, d_model)
```

```python
import functools

import jax
import jax.numpy as jnp
import numpy as np
from jax import lax
from jax.experimental import pallas as pl
from jax.experimental.pallas import tpu as pltpu

F32 = jnp.float32
BF16 = jnp.bfloat16

D_MODEL = 1024
N_GROUPS_A = 8
GROUP_DIM_A = 64
WIDTH_A = N_GROUPS_A * GROUP_DIM_A
CHUNK = 128
N_HEADS_B = 8
HEAD_DIM_B = 64
WIDTH_B = N_HEADS_B * HEAD_DIM_B
MOBA_BLOCK = 256
MOBA_TOPK = 3
ROPE_THETA = 10000.0
D_FF = 2816
CONV_WIDTH = 3
LN_EPS = 1e-5
PROJ_WIDTH = 2 * WIDTH_A + 3 * WIDTH_B + 2 * D_MODEL

LANES = 128
PROJ_ROWS = 512
MERGE_ROWS = 512
FF_TILE = 256
FFN_ROWS = 512
UP_LOOKAHEAD = 3
GATE_CHUNK = 1024
V_EXT_ROWS = HEAD_DIM_B + 16
VMEM_LIMIT = 56 * 1024 * 1024
NEG_INF = float("-inf")
POS_INF = float("inf")
Q_SCALE = np.float32(HEAD_DIM_B ** -0.5 * np.log2(np.e))


def _gelu(x):
    return 0.5 * x * (1.0 + lax.erf(x * np.float32(np.sqrt(0.5))))


def _layer_norm(x, g, b):
    mu = jnp.mean(x, axis=-1, keepdims=True)
    xc = x - mu
    var = jnp.mean(xc * xc, axis=-1, keepdims=True)
    return xc * lax.rsqrt(var + LN_EPS) * g + b


def _dot(a, b):
    return jnp.dot(a, b, preferred_element_type=F32)


def _dot_nt(a, b):
    return lax.dot_general(a, b, (((1,), (1,)), ((), ())), preferred_element_type=F32)


def _dot_tn(a, b):
    return lax.dot_general(a, b, (((0,), (0,)), ((), ())), preferred_element_type=F32)


def _rope(t, cos, sin_signed):
    lane = lax.broadcasted_iota(jnp.int32, (t.shape[0], LANES), 1)
    first_half = (lane % HEAD_DIM_B) < (HEAD_DIM_B // 2)
    out = []
    for c in range(t.shape[1] // LANES):
        blk = t[:, c * LANES:(c + 1) * LANES]
        up = pltpu.roll(blk, LANES - HEAD_DIM_B // 2, 1)
        down = pltpu.roll(blk, HEAD_DIM_B // 2, 1)
        partner = jnp.where(first_half, up, down)
        out.append(blk * cos + partner * sin_signed)
    return jnp.concatenate(out, axis=1)


def _proj_kernel(x_ref, w_ref, cos_ref, sin_ref, lng_ref, lnb_ref, wsp_ref, bsp_ref,
                 ya_ref, q_ref, k_ref, vext_ref, g_ref, wvt_ref):
    @pl.when(pl.program_id(0) == 0)
    def _():
        w_v = w_ref[:, 2 * WIDTH_A + 2 * WIDTH_B:2 * WIDTH_A + 3 * WIDTH_B].astype(F32)
        wvt_ref[...] = w_v.T.astype(BF16)

    xb = x_ref[...].astype(BF16)
    gu = _gelu(_dot(xb, w_ref[:, 0:WIDTH_A]))
    gv = _gelu(_dot(xb, w_ref[:, WIDTH_A:2 * WIDTH_A]))
    g_ref[...] = _dot(xb, w_ref[:, 2 * WIDTH_A + 3 * WIDTH_B:]).astype(BF16)
    cos = cos_ref[...]
    sin = sin_ref[...]
    q = _rope(_dot(xb, w_ref[:, 2 * WIDTH_A:2 * WIDTH_A + WIDTH_B]), cos, sin)
    q_ref[0] = (q * Q_SCALE).astype(BF16)
    k = _rope(_dot(xb, w_ref[:, 2 * WIDTH_A + WIDTH_B:2 * WIDTH_A + 2 * WIDTH_B]), cos, sin)
    k_ref[0] = k.astype(BF16)
    vt = _dot_nt(wvt_ref[...], xb).astype(BF16)
    ones = jnp.ones((V_EXT_ROWS - HEAD_DIM_B, MOBA_BLOCK), BF16)
    for t in range(PROJ_ROWS // MOBA_BLOCK):
        for h in range(N_HEADS_B):
            vext_ref[0, t, h * V_EXT_ROWS:h * V_EXT_ROWS + HEAD_DIM_B, :] = \
                vt[h * HEAD_DIM_B:(h + 1) * HEAD_DIM_B, t * MOBA_BLOCK:(t + 1) * MOBA_BLOCK]
            vext_ref[0, t, h * V_EXT_ROWS + HEAD_DIM_B:(h + 1) * V_EXT_ROWS, :] = ones

    vn = _layer_norm(gv, lng_ref[...], lnb_ref[...]).astype(BF16)
    t_idx = lax.broadcasted_iota(jnp.int32, (CHUNK, 2 * CHUNK), 0)
    s_idx = lax.broadcasted_iota(jnp.int32, (CHUNK, 2 * CHUNK), 1) % CHUNK
    causal = s_idx <= t_idx
    w_pairs = [jnp.where(causal, wsp_ref[p], jnp.zeros((), BF16)) for p in range(N_GROUPS_A // 2)]
    lane = lax.broadcasted_iota(jnp.int32, (CHUNK, LANES), 1)
    left = lane < GROUP_DIM_A
    zero = jnp.zeros((CHUNK, LANES), BF16)
    for c in range(PROJ_ROWS // CHUNK):
        rows = slice(c * CHUNK, (c + 1) * CHUNK)
        mixed = []
        for p in range(N_GROUPS_A // 2):
            blk = vn[rows, p * LANES:(p + 1) * LANES]
            stacked = jnp.concatenate([jnp.where(left, blk, zero), jnp.where(left, zero, blk)], axis=0)
            mixed.append(_dot(w_pairs[p], stacked))
        mixed = jnp.concatenate(mixed, axis=1) + bsp_ref[...]
        ya_ref[rows, :] = (gu[rows, :] * mixed).astype(BF16)


def _projection(x2, w_in_b, cos_t, sin_t, ln_g, ln_b, w_sp, b_sp, bsz, seq):
    n_rows = x2.shape[0]
    tiles_per_seq = seq // PROJ_ROWS
    blocks_per_tile = PROJ_ROWS // MOBA_BLOCK
    const = lambda i: (0, 0)
    resident = pl.Buffered(1)
    seq_tile = lambda i: (i // tiles_per_seq, i % tiles_per_seq, 0)
    return pl.pallas_call(
        _proj_kernel,
        grid=(n_rows // PROJ_ROWS,),
        in_specs=[
            pl.BlockSpec((PROJ_ROWS, D_MODEL), lambda i: (i, 0)),
            pl.BlockSpec((D_MODEL, PROJ_WIDTH), const, pipeline_mode=resident),
            pl.BlockSpec((PROJ_ROWS, LANES), lambda i: (i % tiles_per_seq, 0)),
            pl.BlockSpec((PROJ_ROWS, LANES), lambda i: (i % tiles_per_seq, 0)),
            pl.BlockSpec((1, WIDTH_A), const, pipeline_mode=resident),
            pl.BlockSpec((1, WIDTH_A), const, pipeline_mode=resident),
            pl.BlockSpec((N_GROUPS_A // 2, CHUNK, 2 * CHUNK), lambda i: (0, 0, 0),
                         pipeline_mode=resident),
            pl.BlockSpec((CHUNK, WIDTH_A), const, pipeline_mode=resident),
        ],
        out_specs=[
            pl.BlockSpec((PROJ_ROWS, WIDTH_A), lambda i: (i, 0)),
            pl.BlockSpec((1, PROJ_ROWS, WIDTH_B), seq_tile),
            pl.BlockSpec((1, PROJ_ROWS, WIDTH_B), seq_tile),
            pl.BlockSpec((1, blocks_per_tile, N_HEADS_B * V_EXT_ROWS, MOBA_BLOCK),
                         lambda i: (i // tiles_per_seq, i % tiles_per_seq, 0, 0)),
            pl.BlockSpec((PROJ_ROWS, 2 * D_MODEL), lambda i: (i, 0)),
        ],
        out_shape=[
            jax.ShapeDtypeStruct((n_rows, WIDTH_A), BF16),
            jax.ShapeDtypeStruct((bsz, seq, WIDTH_B), BF16),
            jax.ShapeDtypeStruct((bsz, seq, WIDTH_B), BF16),
            jax.ShapeDtypeStruct((bsz, seq // MOBA_BLOCK, N_HEADS_B * V_EXT_ROWS, MOBA_BLOCK), BF16),
            jax.ShapeDtypeStruct((n_rows, 2 * D_MODEL), BF16),
        ],
        scratch_shapes=[pltpu.VMEM((WIDTH_B, D_MODEL), BF16)],
        compiler_params=pltpu.CompilerParams(
            dimension_semantics=("arbitrary",), vmem_limit_bytes=VMEM_LIMIT),
        name="sgu_moba_projection",
    )(x2, w_in_b, cos_t, sin_t, ln_g, ln_b, w_sp, b_sp)


def _gate_kernel(q_ref, k_ref, pen_ref, *, n_blocks, seq):
    k_mean = jnp.sum(k_ref[0].astype(F32).reshape(n_blocks, MOBA_BLOCK, LANES), axis=1) \
        * np.float32(1.0 / MOBA_BLOCK)
    lane_k = lax.broadcasted_iota(jnp.int32, (n_blocks, LANES), 1)
    blk_row = lax.broadcasted_iota(jnp.int32, (n_blocks, GATE_CHUNK), 0)
    lane_t = lax.broadcasted_iota(jnp.int32, (n_blocks, GATE_CHUNK), 1)
    for h in range(LANES // HEAD_DIM_B):
        km = jnp.where(lane_k // HEAD_DIM_B == h, k_mean, 0.0)
        km_hi = km.astype(BF16)
        km_lo = (km - km_hi.astype(F32)).astype(BF16)
        for c in range(seq // GATE_CHUNK):
            q_c = q_ref[0, c * GATE_CHUNK:(c + 1) * GATE_CHUNK, :]
            gate = _dot_nt(km_hi, q_c) + _dot_nt(km_lo, q_c)
            past = blk_row < (lane_t + c * GATE_CHUNK) // MOBA_BLOCK
            gate = jnp.where(past, gate, NEG_INF)
            sel = jnp.zeros(gate.shape, jnp.bool_)
            for _ in range(MOBA_TOPK):
                best = jnp.max(gate, axis=0, keepdims=True)
                first = jnp.min(jnp.where(gate == best, blk_row, n_blocks), axis=0, keepdims=True)
                pick = blk_row == first
                sel = sel | pick
                gate = jnp.where(pick, NEG_INF, gate)
            pen_ref[0, h, :, c * GATE_CHUNK:(c + 1) * GATE_CHUNK] = \
                jnp.where(sel & past, 0.0, POS_INF).astype(F32)


def _moba_gate(q, k, bsz, seq):
    n_blocks = seq // MOBA_BLOCK
    return pl.pallas_call(
        functools.partial(_gate_kernel, n_blocks=n_blocks, seq=seq),
        grid=(bsz, WIDTH_B // LANES),
        in_specs=[
            pl.BlockSpec((1, seq, LANES), lambda b, p: (b, 0, p)),
            pl.BlockSpec((1, seq, LANES), lambda b, p: (b, 0, p)),
        ],
        out_specs=pl.BlockSpec((1, LANES // HEAD_DIM_B, n_blocks, seq), lambda b, p: (b, p, 0, 0)),
        out_shape=jax.ShapeDtypeStruct((bsz, N_HEADS_B, n_blocks, seq), F32),
        compiler_params=pltpu.CompilerParams(
            dimension_semantics=("arbitrary", "arbitrary"), vmem_limit_bytes=VMEM_LIMIT),
        name="moba_gate",
    )(q, k)


def _attn_kernel(q_ref, k_ref, vext_ref, pen_ref, o_ref, qm_ref, m_ref, acc_ref,
                 sa_ref, samax_ref, sb_ref, sbmax_ref):
    i = pl.program_id(1)
    pair_heads = LANES // HEAD_DIM_B
    n_pairs = N_HEADS_B // pair_heads

    row0 = pl.multiple_of(i * MOBA_BLOCK, MOBA_BLOCK)
    lane_q = lax.broadcasted_iota(jnp.int32, (MOBA_BLOCK, LANES), 1)
    key_pos = lax.broadcasted_iota(jnp.int32, (MOBA_BLOCK, MOBA_BLOCK), 0)
    qry_pos = lax.broadcasted_iota(jnp.int32, (MOBA_BLOCK, MOBA_BLOCK), 1)

    q_pairs = [q_ref[0, :, p * LANES:(p + 1) * LANES] for p in range(n_pairs)]
    k_own = [k_ref[0, pl.ds(row0, MOBA_BLOCK), p * LANES:(p + 1) * LANES] for p in range(n_pairs)]
    v_own = [vext_ref[0, i, h * V_EXT_ROWS:(h + 1) * V_EXT_ROWS, :] for h in range(N_HEADS_B)]
    qm_all, s_all = [], []
    for h in range(N_HEADS_B):
        in_head = lane_q // HEAD_DIM_B == h % pair_heads
        qm = jnp.where(in_head, q_pairs[h // pair_heads], jnp.zeros((), BF16))
        qm_all.append(qm)
        s_all.append(_dot_nt(k_own[h // pair_heads], qm))
    k_first = [k_ref[0, 0:MOBA_BLOCK, p * LANES:(p + 1) * LANES] for p in range(n_pairs)]
    m_all, acc_all, s0_all = [], [], []
    for h in range(N_HEADS_B):
        s0_all.append(_dot_nt(k_first[h // pair_heads], qm_all[h]))
        s = jnp.where(key_pos <= qry_pos, s_all[h], NEG_INF)
        m0 = jnp.max(s, axis=0, keepdims=True)
        m_all.append(m0)
        acc_all.append(_dot(v_own[h], jnp.exp2(s - m0).astype(BF16)))
    for h in range(N_HEADS_B):
        qm_ref[h] = qm_all[h]
        m_ref[h] = m_all[h]
        acc_ref[h] = acc_all[h]
        sa_ref[h] = s0_all[h]
        samax_ref[h] = jnp.max(s0_all[h], axis=0, keepdims=True)

    def load_keys(j):
        col0 = pl.multiple_of(j * MOBA_BLOCK, MOBA_BLOCK)
        return [k_ref[0, pl.ds(col0, MOBA_BLOCK), p * LANES:(p + 1) * LANES] for p in range(n_pairs)]

    def load_block(j):
        v_all = [vext_ref[0, j, h * V_EXT_ROWS:(h + 1) * V_EXT_ROWS, :] for h in range(N_HEADS_B)]
        pen_all = [pen_ref[0, h, pl.ds(j, 1), :] for h in range(N_HEADS_B)]
        return v_all, pen_all

    def absorb_head(h, v_h, pen_h, s_ref, smax_ref):
        m_run = m_ref[h]
        m_new = jnp.maximum(m_run, smax_ref[h] - pen_h)
        p_j = jnp.exp2(s_ref[h] - (m_new + pen_h)).astype(BF16)
        m_ref[h] = m_new
        acc_ref[h] = jnp.exp2(m_run - m_new) * acc_ref[h] + _dot(v_h, p_j)

    def fused(k_next, s_next_ref, smax_next_ref, block, s_ref, smax_ref):
        v_all, pen_all = block
        for h in range(N_HEADS_B):
            s_new = _dot_nt(k_next[h // pair_heads], qm_ref[h])
            s_next_ref[h] = s_new
            smax_next_ref[h] = jnp.max(s_new, axis=0, keepdims=True)
            absorb_head(h, v_all[h], pen_all[h], s_ref, smax_ref)

    def kv_pair(jj, carry):
        j = 2 * jj
        keys_b, keys_a = load_keys(j + 1), load_keys(j + 2)
        block_a, block_b = load_block(j), load_block(j + 1)
        fused(keys_b, sb_ref, sbmax_ref, block_a, sa_ref, samax_ref)
        fused(keys_a, sa_ref, samax_ref, block_b, sb_ref, sbmax_ref)
        return carry

    lax.fori_loop(0, i // 2, kv_pair, 0)

    @pl.when(i % 2 == 1)
    def _():
        v_all, pen_all = load_block(i - 1)
        for h in range(N_HEADS_B):
            absorb_head(h, v_all[h], pen_all[h], sa_ref, samax_ref)

    for h in range(N_HEADS_B):
        acc = acc_ref[h]
        out = acc[0:HEAD_DIM_B, :] / acc[HEAD_DIM_B:HEAD_DIM_B + 1, :]
        o_ref[0, 0, h * HEAD_DIM_B:(h + 1) * HEAD_DIM_B, :] = out.astype(BF16)


def _moba_attention(q, k, vext, pen, bsz, seq):
    n_blocks = seq // MOBA_BLOCK
    return pl.pallas_call(
        _attn_kernel,
        grid=(bsz, n_blocks),
        in_specs=[
            pl.BlockSpec((1, MOBA_BLOCK, WIDTH_B), lambda b, i: (b, i, 0)),
            pl.BlockSpec((1, seq, WIDTH_B), lambda b, i: (b, 0, 0)),
            pl.BlockSpec((1, n_blocks, N_HEADS_B * V_EXT_ROWS, MOBA_BLOCK), lambda b, i: (b, 0, 0, 0)),
            pl.BlockSpec((1, N_HEADS_B, n_blocks, MOBA_BLOCK), lambda b, i: (b, 0, 0, i)),
        ],
        out_specs=pl.BlockSpec((1, 1, WIDTH_B, MOBA_BLOCK), lambda b, i: (b, i, 0, 0)),
        out_shape=jax.ShapeDtypeStruct((bsz, n_blocks, WIDTH_B, MOBA_BLOCK), BF16),
        scratch_shapes=[
            pltpu.VMEM((N_HEADS_B, MOBA_BLOCK, LANES), BF16),
            pltpu.VMEM((N_HEADS_B, 1, MOBA_BLOCK), F32),
            pltpu.VMEM((N_HEADS_B, V_EXT_ROWS, MOBA_BLOCK), F32),
            pltpu.VMEM((N_HEADS_B, MOBA_BLOCK, MOBA_BLOCK), F32),
            pltpu.VMEM((N_HEADS_B, 1, MOBA_BLOCK), F32),
            pltpu.VMEM((N_HEADS_B, MOBA_BLOCK, MOBA_BLOCK), F32),
            pltpu.VMEM((N_HEADS_B, 1, MOBA_BLOCK), F32),
        ],
        compiler_params=pltpu.CompilerParams(
            dimension_semantics=("arbitrary", "arbitrary"), vmem_limit_bytes=VMEM_LIMIT),
        name="moba_attention",
    )(q, k, vext, pen)


def _merge_kernel(x_ref, ya_ref, ybt_ref, g_ref, bg_ref, wa_ref, wb_ref, wo_ref, lng_ref, lnb_ref,
                  o_ref, *, alpha):
    branch_a = _dot(ya_ref[...], wa_ref[...])
    branch_b = jnp.concatenate(
        [_dot_tn(ybt_ref[0, t], wb_ref[...]) for t in range(MERGE_ROWS // MOBA_BLOCK)], axis=0)
    g = g_ref[...].astype(F32) + bg_ref[...]
    merged = (jax.nn.sigmoid(g[:, :D_MODEL]) * branch_a
              + jax.nn.sigmoid(g[:, D_MODEL:]) * branch_b)
    mix = _dot(merged.astype(BF16), wo_ref[...])
    o_ref[...] = _layer_norm(alpha * x_ref[...] + mix, lng_ref[...], lnb_ref[...])


def _merge(x2, ya, ybt, g, b_gate, w_a, w_b, w_o, ln_g, ln_b, alpha, seq):
    n_rows = x2.shape[0]
    tiles_per_seq = seq // MERGE_ROWS
    const = lambda i: (0, 0)
    resident = pl.Buffered(1)
    return pl.pallas_call(
        functools.partial(_merge_kernel, alpha=alpha),
        grid=(n_rows // MERGE_ROWS,),
        in_specs=[
            pl.BlockSpec((MERGE_ROWS, D_MODEL), lambda i: (i, 0)),
            pl.BlockSpec((MERGE_ROWS, WIDTH_A), lambda i: (i, 0)),
            pl.BlockSpec((1, MERGE_ROWS // MOBA_BLOCK, WIDTH_B, MOBA_BLOCK),
                         lambda i: (i // tiles_per_seq, i % tiles_per_seq, 0, 0)),
            pl.BlockSpec((MERGE_ROWS, 2 * D_MODEL), lambda i: (i, 0)),
            pl.BlockSpec((1, 2 * D_MODEL), const, pipeline_mode=resident),
            pl.BlockSpec((WIDTH_A, D_MODEL), const, pipeline_mode=resident),
            pl.BlockSpec((WIDTH_B, D_MODEL), const, pipeline_mode=resident),
            pl.BlockSpec((D_MODEL, D_MODEL), const, pipeline_mode=resident),
            pl.BlockSpec((1, D_MODEL), const, pipeline_mode=resident),
            pl.BlockSpec((1, D_MODEL), const, pipeline_mode=resident),
        ],
        out_specs=pl.BlockSpec((MERGE_ROWS, D_MODEL), lambda i: (i, 0)),
        out_shape=jax.ShapeDtypeStruct((n_rows, D_MODEL), F32),
        compiler_params=pltpu.CompilerParams(
            dimension_semantics=("arbitrary",), vmem_limit_bytes=VMEM_LIMIT),
        name="merge_outproj_ln",
    )(x2, ya, ybt, g, b_gate, w_a, w_b, w_o, ln_g, ln_b)


def _ffn_kernel(x_ref, wup_ref, cw_ref, cb_ref, wdn_ref, lng_ref, lnb_ref, o_ref,
                hbuf_ref, carry_ref, acc_ref, *, alpha, tiles_per_seq):
    i = pl.program_id(0)

    @pl.when(i % tiles_per_seq == 0)
    def _():
        carry_ref[...] = jnp.zeros(carry_ref.shape, F32)

    x = x_ref[...]
    xb = x.astype(BF16)
    acc_ref[...] = jnp.zeros(acc_ref.shape, F32)
    pad = carry_ref.shape[0]
    n_steps = D_FF // FF_TILE

    def cols_of(j, half):
        c0 = half * D_FF + j * FF_TILE
        return slice(c0, c0 + FF_TILE)

    def up_proj(j):
        for half in range(2):
            cols = cols_of(j, half)
            hbuf = hbuf_ref.at[(2 * j + half) % hbuf_ref.shape[0]]
            hbuf[0:pad, :] = carry_ref[:, cols]
            h = _dot(xb, wup_ref[:, cols])
            hbuf[pad:, :] = h
            carry_ref[:, cols] = h[FFN_ROWS - pad:, :]

    def conv_act(j):
        conv = []
        for half in range(2):
            cols = cols_of(j, half)
            hbuf = hbuf_ref.at[(2 * j + half) % hbuf_ref.shape[0]]
            cw = cw_ref[:, cols]
            conv.append(cw[0:1, :] * hbuf[pad - 2:pad - 2 + FFN_ROWS, :]
                        + cw[1:2, :] * hbuf[pad - 1:pad - 1 + FFN_ROWS, :]
                        + cw[2:3, :] * hbuf[pad:, :] + cb_ref[:, cols])
        return (_gelu(conv[0]) * conv[1]).astype(BF16)

    for j in range(min(UP_LOOKAHEAD, n_steps)):
        up_proj(j)
    for j in range(n_steps):
        if j + UP_LOOKAHEAD < n_steps:
            up_proj(j + UP_LOOKAHEAD)
        acc_ref[...] += _dot(conv_act(j), wdn_ref[j * FF_TILE:(j + 1) * FF_TILE, :])
    o_ref[...] = _layer_norm(alpha * x + acc_ref[...], lng_ref[...], lnb_ref[...])


def _conv_ffn(x1, w_up, conv_w, conv_b, w_dn, ln_g, ln_b, alpha, seq):
    n_rows = x1.shape[0]
    tiles_per_seq = seq // FFN_ROWS
    const = lambda i: (0, 0)
    resident = pl.Buffered(1)
    pad = 8
    return pl.pallas_call(
        functools.partial(_ffn_kernel, alpha=alpha, tiles_per_seq=tiles_per_seq),
        grid=(n_rows // FFN_ROWS,),
        in_specs=[
            pl.BlockSpec((FFN_ROWS, D_MODEL), lambda i: (i, 0)),
            pl.BlockSpec((D_MODEL, 2 * D_FF), const, pipeline_mode=resident),
            pl.BlockSpec((CONV_WIDTH, 2 * D_FF), const, pipeline_mode=resident),
            pl.BlockSpec((1, 2 * D_FF), const, pipeline_mode=resident),
            pl.BlockSpec((D_FF, D_MODEL), const, pipeline_mode=resident),
            pl.BlockSpec((1, D_MODEL), const, pipeline_mode=resident),
            pl.BlockSpec((1, D_MODEL), const, pipeline_mode=resident),
        ],
        out_specs=pl.BlockSpec((FFN_ROWS, D_MODEL), lambda i: (i, 0)),
        out_shape=jax.ShapeDtypeStruct((n_rows, D_MODEL), F32),
        scratch_shapes=[
            pltpu.VMEM((2 * (UP_LOOKAHEAD + 1), pad + FFN_ROWS, FF_TILE), F32),
            pltpu.VMEM((pad, 2 * D_FF), F32),
            pltpu.VMEM((FFN_ROWS, D_MODEL), F32),
        ],
        compiler_params=pltpu.CompilerParams(
            dimension_semantics=("arbitrary",), vmem_limit_bytes=VMEM_LIMIT),
        name="conv_ffn_ln",
    )(x1, w_up, conv_w, conv_b, w_dn, ln_g, ln_b)


def _rope_tables(seq):
    half = HEAD_DIM_B // 2
    inv_freq = np.float32(ROPE_THETA) ** (-np.arange(half, dtype=np.float32) / np.float32(half))
    ang = np.arange(seq, dtype=np.float32)[:, None] * inv_freq[None, :]
    cos = np.tile(np.cos(ang), (1, LANES // half)).astype(np.float32)
    sin = np.tile(np.sin(ang), (1, LANES // half)).astype(np.float32)
    first_half = (np.arange(LANES) % HEAD_DIM_B) < half
    return jnp.asarray(cos), jnp.asarray(np.where(first_half[None, :], -sin, sin))


def kernel(x, w_in, b_gate, sgu_ln_g, sgu_ln_b, w_spatial, b_spatial, w_branch_a, w_branch_b,
           w_out, ln1_g, ln1_b, w_up, conv_w, conv_b, w_down, ln2_g, ln2_b):
    bsz, seq, d_model = x.shape
    depth = w_in.shape[0]
    assert d_model == D_MODEL and all(seq % t == 0 for t in (PROJ_ROWS, MERGE_ROWS, FFN_ROWS))
    assert all(t % MOBA_BLOCK == 0 for t in (PROJ_ROWS, MERGE_ROWS))
    alpha = float((2.0 * depth) ** 0.25)
    cos_t, sin_t = _rope_tables(seq)
    x2 = x.reshape(bsz * seq, d_model)
    for l in range(depth):
        w_in_b = w_in[l].astype(BF16)
        w_sp = w_spatial[l].astype(BF16).reshape(N_GROUPS_A // 2, 2, CHUNK, CHUNK)
        w_sp = w_sp.transpose(0, 2, 1, 3).reshape(N_GROUPS_A // 2, CHUNK, 2 * CHUNK)
        b_sp = jnp.repeat(b_spatial[l].T, GROUP_DIM_A, axis=1)
        ya, q, k, vext, g = _projection(
            x2, w_in_b, cos_t, sin_t, sgu_ln_g[l][None, :], sgu_ln_b[l][None, :],
            w_sp, b_sp, bsz, seq)
        pen = _moba_gate(q, k, bsz, seq)
        ybt = _moba_attention(q, k, vext, pen, bsz, seq)
        x1 = _merge(x2, ya, ybt, g, b_gate[l][None, :], w_branch_a[l].astype(BF16),
                    w_branch_b[l].astype(BF16), w_out[l].astype(BF16),
                    ln1_g[l][None, :], ln1_b[l][None, :], alpha, seq)
        x2 = _conv_ffn(x1, w_up[l].astype(BF16), conv_w[l], conv_b[l][None, :],
                       w_down[l].astype(BF16), ln2_g[l][None, :], ln2_b[l][None, :], alpha, seq)
    return x2.reshape(bsz, seq, d_model)
```

```python
import functools

import jax
import jax.numpy as jnp
import numpy as np
from jax import lax
from jax.experimental import pallas as pl
from jax.experimental.pallas import tpu as pltpu

F32 = jnp.float32
BF16 = jnp.bfloat16

D_MODEL = 1024
N_GROUPS_A = 8
GROUP_DIM_A = 64
WIDTH_A = N_GROUPS_A * GROUP_DIM_A
CHUNK = 128
N_HEADS_B = 8
HEAD_DIM_B = 64
WIDTH_B = N_HEADS_B * HEAD_DIM_B
MOBA_BLOCK = 256
MOBA_TOPK = 3
ROPE_THETA = 10000.0
D_FF = 2816
CONV_WIDTH = 3
LN_EPS = 1e-5
PROJ_WIDTH = 2 * WIDTH_A + 3 * WIDTH_B + 2 * D_MODEL

LANES = 128
PROJ_ROWS = 1024
MERGE_ROWS = 1024
FF_TILE = 256
FFN_ROWS = 512
UP_LOOKAHEAD = 3
GATE_CHUNK = 1024
V_EXT_ROWS = HEAD_DIM_B + 16
VMEM_LIMIT = 56 * 1024 * 1024
NEG_INF = float("-inf")
POS_INF = float("inf")
Q_SCALE = np.float32(HEAD_DIM_B ** -0.5 * np.log2(np.e))


def _gelu(x):
    return 0.5 * x * (1.0 + lax.erf(x * np.float32(np.sqrt(0.5))))


def _layer_norm(x, g, b):
    mu = jnp.mean(x, axis=-1, keepdims=True)
    xc = x - mu
    var = jnp.mean(xc * xc, axis=-1, keepdims=True)
    return xc * lax.rsqrt(var + LN_EPS) * g + b


def _dot(a, b):
    return jnp.dot(a, b, preferred_element_type=F32)


def _dot_nt(a, b):
    return lax.dot_general(a, b, (((1,), (1,)), ((), ())), preferred_element_type=F32)


def _dot_tn(a, b):
    return lax.dot_general(a, b, (((0,), (0,)), ((), ())), preferred_element_type=F32)


def _rope(t, cos, sin_signed):
    lane = lax.broadcasted_iota(jnp.int32, (t.shape[0], LANES), 1)
    first_half = (lane % HEAD_DIM_B) < (HEAD_DIM_B // 2)
    out = []
    for c in range(t.shape[1] // LANES):
        blk = t[:, c * LANES:(c + 1) * LANES]
        up = pltpu.roll(blk, LANES - HEAD_DIM_B // 2, 1)
        down = pltpu.roll(blk, HEAD_DIM_B // 2, 1)
        partner = jnp.where(first_half, up, down)
        out.append(blk * cos + partner * sin_signed)
    return jnp.concatenate(out, axis=1)


def _proj_kernel(x_ref, w_ref, cos_ref, sin_ref, lng_ref, lnb_ref, wsp_ref, bsp_ref,
                 ya_ref, q_ref, k_ref, vext_ref, g_ref, wvt_ref):
    @pl.when(pl.program_id(0) == 0)
    def _():
        w_v = w_ref[:, 2 * WIDTH_A + 2 * WIDTH_B:2 * WIDTH_A + 3 * WIDTH_B].astype(F32)
        wvt_ref[...] = w_v.T.astype(BF16)

    xb = x_ref[...].astype(BF16)
    gu = _gelu(_dot(xb, w_ref[:, 0:WIDTH_A]))
    gv = _gelu(_dot(xb, w_ref[:, WIDTH_A:2 * WIDTH_A]))
    g_ref[...] = _dot(xb, w_ref[:, 2 * WIDTH_A + 3 * WIDTH_B:]).astype(BF16)
    cos = cos_ref[...]
    sin = sin_ref[...]
    q = _rope(_dot(xb, w_ref[:, 2 * WIDTH_A:2 * WIDTH_A + WIDTH_B]), cos, sin)
    q_ref[0] = (q * Q_SCALE).astype(BF16)
    k = _rope(_dot(xb, w_ref[:, 2 * WIDTH_A + WIDTH_B:2 * WIDTH_A + 2 * WIDTH_B]), cos, sin)
    k_ref[0] = k.astype(BF16)
    vt = _dot_nt(wvt_ref[...], xb).astype(BF16)
    ones = jnp.ones((V_EXT_ROWS - HEAD_DIM_B, MOBA_BLOCK), BF16)
    for t in range(PROJ_ROWS // MOBA_BLOCK):
        for h in range(N_HEADS_B):
            vext_ref[0, t, h * V_EXT_ROWS:h * V_EXT_ROWS + HEAD_DIM_B, :] = \
                vt[h * HEAD_DIM_B:(h + 1) * HEAD_DIM_B, t * MOBA_BLOCK:(t + 1) * MOBA_BLOCK]
            vext_ref[0, t, h * V_EXT_ROWS + HEAD_DIM_B:(h + 1) * V_EXT_ROWS, :] = ones

    vn = _layer_norm(gv, lng_ref[...], lnb_ref[...]).astype(BF16)
    t_idx = lax.broadcasted_iota(jnp.int32, (CHUNK, 2 * CHUNK), 0)
    s_idx = lax.broadcasted_iota(jnp.int32, (CHUNK, 2 * CHUNK), 1) % CHUNK
    causal = s_idx <= t_idx
    w_pairs = [jnp.where(causal, wsp_ref[p], jnp.zeros((), BF16)) for p in range(N_GROUPS_A // 2)]
    lane = lax.broadcasted_iota(jnp.int32, (CHUNK, LANES), 1)
    left = lane < GROUP_DIM_A
    zero = jnp.zeros((CHUNK, LANES), BF16)
    for c in range(PROJ_ROWS // CHUNK):
        rows = slice(c * CHUNK, (c + 1) * CHUNK)
        mixed = []
        for p in range(N_GROUPS_A // 2):
            blk = vn[rows, p * LANES:(p + 1) * LANES]
            stacked = jnp.concatenate([jnp.where(left, blk, zero), jnp.where(left, zero, blk)], axis=0)
            mixed.append(_dot(w_pairs[p], stacked))
        mixed = jnp.concatenate(mixed, axis=1) + bsp_ref[...]
        ya_ref[rows, :] = (gu[rows, :] * mixed).astype(BF16)


def _projection(x2, w_in_b, cos_t, sin_t, ln_g, ln_b, w_sp, b_sp, bsz, seq):
    n_rows = x2.shape[0]
    tiles_per_seq = seq // PROJ_ROWS
    blocks_per_tile = PROJ_ROWS // MOBA_BLOCK
    const = lambda i: (0, 0)
    resident = pl.Buffered(1)
    seq_tile = lambda i: (i // tiles_per_seq, i % tiles_per_seq, 0)
    return pl.pallas_call(
        _proj_kernel,
        grid=(n_rows // PROJ_ROWS,),
        in_specs=[
            pl.BlockSpec((PROJ_ROWS, D_MODEL), lambda i: (i, 0)),
            pl.BlockSpec((D_MODEL, PROJ_WIDTH), const, pipeline_mode=resident),
            pl.BlockSpec((PROJ_ROWS, LANES), lambda i: (i % tiles_per_seq, 0)),
            pl.BlockSpec((PROJ_ROWS, LANES), lambda i: (i % tiles_per_seq, 0)),
            pl.BlockSpec((1, WIDTH_A), const, pipeline_mode=resident),
            pl.BlockSpec((1, WIDTH_A), const, pipeline_mode=resident),
            pl.BlockSpec((N_GROUPS_A // 2, CHUNK, 2 * CHUNK), lambda i: (0, 0, 0),
                         pipeline_mode=resident),
            pl.BlockSpec((CHUNK, WIDTH_A), const, pipeline_mode=resident),
        ],
        out_specs=[
            pl.BlockSpec((PROJ_ROWS, WIDTH_A), lambda i: (i, 0)),
            pl.BlockSpec((1, PROJ_ROWS, WIDTH_B), seq_tile),
            pl.BlockSpec((1, PROJ_ROWS, WIDTH_B), seq_tile),
            pl.BlockSpec((1, blocks_per_tile, N_HEADS_B * V_EXT_ROWS, MOBA_BLOCK),
                         lambda i: (i // tiles_per_seq, i % tiles_per_seq, 0, 0)),
            pl.BlockSpec((PROJ_ROWS, 2 * D_MODEL), lambda i: (i, 0)),
        ],
        out_shape=[
            jax.ShapeDtypeStruct((n_rows, WIDTH_A), BF16),
            jax.ShapeDtypeStruct((bsz, seq, WIDTH_B), BF16),
            jax.ShapeDtypeStruct((bsz, seq, WIDTH_B), BF16),
            jax.ShapeDtypeStruct((bsz, seq // MOBA_BLOCK, N_HEADS_B * V_EXT_ROWS, MOBA_BLOCK), BF16),
            jax.ShapeDtypeStruct((n_rows, 2 * D_MODEL), BF16),
        ],
        scratch_shapes=[pltpu.VMEM((WIDTH_B, D_MODEL), BF16)],
        compiler_params=pltpu.CompilerParams(
            dimension_semantics=("arbitrary",), vmem_limit_bytes=VMEM_LIMIT),
        name="sgu_moba_projection",
    )(x2, w_in_b, cos_t, sin_t, ln_g, ln_b, w_sp, b_sp)


def _gate_kernel(q_ref, k_ref, pen_ref, *, n_blocks, seq):
    k_mean = jnp.sum(k_ref[0].astype(F32).reshape(n_blocks, MOBA_BLOCK, LANES), axis=1) \
        * np.float32(1.0 / MOBA_BLOCK)
    lane_k = lax.broadcasted_iota(jnp.int32, (n_blocks, LANES), 1)
    blk_row = lax.broadcasted_iota(jnp.int32, (n_blocks, GATE_CHUNK), 0)
    lane_t = lax.broadcasted_iota(jnp.int32, (n_blocks, GATE_CHUNK), 1)
    for h in range(LANES // HEAD_DIM_B):
        km = jnp.where(lane_k // HEAD_DIM_B == h, k_mean, 0.0)
        km_hi = km.astype(BF16)
        km_lo = (km - km_hi.astype(F32)).astype(BF16)
        for c in range(seq // GATE_CHUNK):
            q_c = q_ref[0, c * GATE_CHUNK:(c + 1) * GATE_CHUNK, :]
            gate = _dot_nt(km_hi, q_c) + _dot_nt(km_lo, q_c)
            past = blk_row < (lane_t + c * GATE_CHUNK) // MOBA_BLOCK
            gate = jnp.where(past, gate, NEG_INF)
            sel = jnp.zeros(gate.shape, jnp.bool_)
            for _ in range(MOBA_TOPK):
                best = jnp.max(gate, axis=0, keepdims=True)
                first = jnp.min(jnp.where(gate == best, blk_row, n_blocks), axis=0, keepdims=True)
                pick = blk_row == first
                sel = sel | pick
                gate = jnp.where(pick, NEG_INF, gate)
            pen_ref[0, h, :, c * GATE_CHUNK:(c + 1) * GATE_CHUNK] = \
                jnp.where(sel & past, 0.0, POS_INF).astype(F32)


def _moba_gate(q, k, bsz, seq):
    n_blocks = seq // MOBA_BLOCK
    return pl.pallas_call(
        functools.partial(_gate_kernel, n_blocks=n_blocks, seq=seq),
        grid=(bsz, WIDTH_B // LANES),
        in_specs=[
            pl.BlockSpec((1, seq, LANES), lambda b, p: (b, 0, p)),
            pl.BlockSpec((1, seq, LANES), lambda b, p: (b, 0, p)),
        ],
        out_specs=pl.BlockSpec((1, LANES // HEAD_DIM_B, n_blocks, seq), lambda b, p: (b, p, 0, 0)),
        out_shape=jax.ShapeDtypeStruct((bsz, N_HEADS_B, n_blocks, seq), F32),
        compiler_params=pltpu.CompilerParams(
            dimension_semantics=("arbitrary", "arbitrary"), vmem_limit_bytes=VMEM_LIMIT),
        name="moba_gate",
    )(q, k)


def _attn_kernel(q_ref, k_ref, vext_ref, pen_ref, o_ref, qm_ref, m_ref, acc_ref,
                 sa_ref, samax_ref, sb_ref, sbmax_ref):
    i = pl.program_id(1)
    pair_heads = LANES // HEAD_DIM_B
    n_pairs = N_HEADS_B // pair_heads

    row0 = pl.multiple_of(i * MOBA_BLOCK, MOBA_BLOCK)
    lane_q = lax.broadcasted_iota(jnp.int32, (MOBA_BLOCK, LANES), 1)
    key_pos = lax.broadcasted_iota(jnp.int32, (MOBA_BLOCK, MOBA_BLOCK), 0)
    qry_pos = lax.broadcasted_iota(jnp.int32, (MOBA_BLOCK, MOBA_BLOCK), 1)

    q_pairs = [q_ref[0, :, p * LANES:(p + 1) * LANES] for p in range(n_pairs)]
    k_own = [k_ref[0, pl.ds(row0, MOBA_BLOCK), p * LANES:(p + 1) * LANES] for p in range(n_pairs)]
    v_own = [vext_ref[0, i, h * V_EXT_ROWS:(h + 1) * V_EXT_ROWS, :] for h in range(N_HEADS_B)]
    qm_all, s_all = [], []
    for h in range(N_HEADS_B):
        in_head = lane_q // HEAD_DIM_B == h % pair_heads
        qm = jnp.where(in_head, q_pairs[h // pair_heads], jnp.zeros((), BF16))
        qm_all.append(qm)
        s_all.append(_dot_nt(k_own[h // pair_heads], qm))
    k_first = [k_ref[0, 0:MOBA_BLOCK, p * LANES:(p + 1) * LANES] for p in range(n_pairs)]
    m_all, acc_all, s0_all = [], [], []
    for h in range(N_HEADS_B):
        s0_all.append(_dot_nt(k_first[h // pair_heads], qm_all[h]))
        s = jnp.where(key_pos <= qry_pos, s_all[h], NEG_INF)
        m0 = jnp.max(s, axis=0, keepdims=True)
        m_all.append(m0)
        acc_all.append(_dot(v_own[h], jnp.exp2(s - m0).astype(BF16)))
    for h in range(N_HEADS_B):
        qm_ref[h] = qm_all[h]
        m_ref[h] = m_all[h]
        acc_ref[h] = acc_all[h]
        sa_ref[h] = s0_all[h]
        samax_ref[h] = jnp.max(s0_all[h], axis=0, keepdims=True)

    def load_keys(j):
        col0 = pl.multiple_of(j * MOBA_BLOCK, MOBA_BLOCK)
        return [k_ref[0, pl.ds(col0, MOBA_BLOCK), p * LANES:(p + 1) * LANES] for p in range(n_pairs)]

    def load_block(j):
        v_all = [vext_ref[0, j, h * V_EXT_ROWS:(h + 1) * V_EXT_ROWS, :] for h in range(N_HEADS_B)]
        pen_all = [pen_ref[0, h, pl.ds(j, 1), :] for h in range(N_HEADS_B)]
        return v_all, pen_all

    def absorb_head(h, v_h, pen_h, s_ref, smax_ref):
        m_run = m_ref[h]
        m_new = jnp.maximum(m_run, smax_ref[h] - pen_h)
        p_j = jnp.exp2(s_ref[h] - (m_new + pen_h)).astype(BF16)
        m_ref[h] = m_new
        acc_ref[h] = jnp.exp2(m_run - m_new) * acc_ref[h] + _dot(v_h, p_j)

    def fused(k_next, s_next_ref, smax_next_ref, block, s_ref, smax_ref):
        v_all, pen_all = block
        for h in range(N_HEADS_B):
            s_new = _dot_nt(k_next[h // pair_heads], qm_ref[h])
            s_next_ref[h] = s_new
            smax_next_ref[h] = jnp.max(s_new, axis=0, keepdims=True)
            absorb_head(h, v_all[h], pen_all[h], s_ref, smax_ref)

    def kv_pair(jj, carry):
        j = 2 * jj
        keys_b, keys_a = load_keys(j + 1), load_keys(j + 2)
        block_a, block_b = load_block(j), load_block(j + 1)
        fused(keys_b, sb_ref, sbmax_ref, block_a, sa_ref, samax_ref)
        fused(keys_a, sa_ref, samax_ref, block_b, sb_ref, sbmax_ref)
        return carry

    def kv_quad(jq, carry):
        kv_pair(2 * jq, carry)
        kv_pair(2 * jq + 1, carry)
        return carry

    def kv_oct(jo, carry):
        kv_quad(2 * jo, carry)
        kv_quad(2 * jo + 1, carry)
        return carry

    lax.fori_loop(0, i // 8, kv_oct, 0)
    lax.fori_loop(i // 8 * 2, i // 4, kv_quad, 0)
    lax.fori_loop(i // 4 * 2, i // 2, kv_pair, 0)

    @pl.when(i % 2 == 1)
    def _():
        v_all, pen_all = load_block(i - 1)
        for h in range(N_HEADS_B):
            absorb_head(h, v_all[h], pen_all[h], sa_ref, samax_ref)

    for h in range(N_HEADS_B):
        acc = acc_ref[h]
        out = acc[0:HEAD_DIM_B, :] / acc[HEAD_DIM_B:HEAD_DIM_B + 1, :]
        o_ref[0, 0, h * HEAD_DIM_B:(h + 1) * HEAD_DIM_B, :] = out.astype(BF16)


def _moba_attention(q, k, vext, pen, bsz, seq):
    n_blocks = seq // MOBA_BLOCK
    return pl.pallas_call(
        _attn_kernel,
        grid=(bsz, n_blocks),
        in_specs=[
            pl.BlockSpec((1, MOBA_BLOCK, WIDTH_B), lambda b, i: (b, i, 0)),
            pl.BlockSpec((1, seq, WIDTH_B), lambda b, i: (b, 0, 0)),
            pl.BlockSpec((1, n_blocks, N_HEADS_B * V_EXT_ROWS, MOBA_BLOCK), lambda b, i: (b, 0, 0, 0)),
            pl.BlockSpec((1, N_HEADS_B, n_blocks, MOBA_BLOCK), lambda b, i: (b, 0, 0, i)),
        ],
        out_specs=pl.BlockSpec((1, 1, WIDTH_B, MOBA_BLOCK), lambda b, i: (b, i, 0, 0)),
        out_shape=jax.ShapeDtypeStruct((bsz, n_blocks, WIDTH_B, MOBA_BLOCK), BF16),
        scratch_shapes=[
            pltpu.VMEM((N_HEADS_B, MOBA_BLOCK, LANES), BF16),
            pltpu.VMEM((N_HEADS_B, 1, MOBA_BLOCK), F32),
            pltpu.VMEM((N_HEADS_B, V_EXT_ROWS, MOBA_BLOCK), F32),
            pltpu.VMEM((N_HEADS_B, MOBA_BLOCK, MOBA_BLOCK), F32),
            pltpu.VMEM((N_HEADS_B, 1, MOBA_BLOCK), F32),
            pltpu.VMEM((N_HEADS_B, MOBA_BLOCK, MOBA_BLOCK), F32),
            pltpu.VMEM((N_HEADS_B, 1, MOBA_BLOCK), F32),
        ],
        compiler_params=pltpu.CompilerParams(
            dimension_semantics=("arbitrary", "arbitrary"), vmem_limit_bytes=VMEM_LIMIT),
        name="moba_attention",
    )(q, k, vext, pen)


def _merge_kernel(x_ref, ya_ref, ybt_ref, g_ref, bg_ref, wa_ref, wb_ref, wo_ref, lng_ref, lnb_ref,
                  o_ref, *, alpha):
    branch_a = _dot(ya_ref[...], wa_ref[...])
    branch_b = jnp.concatenate(
        [_dot_tn(ybt_ref[0, t], wb_ref[...]) for t in range(MERGE_ROWS // MOBA_BLOCK)], axis=0)
    g = g_ref[...].astype(F32) + bg_ref[...]
    merged = (jax.nn.sigmoid(g[:, :D_MODEL]) * branch_a
              + jax.nn.sigmoid(g[:, D_MODEL:]) * branch_b)
    mix = _dot(merged.astype(BF16), wo_ref[...])
    o_ref[...] = _layer_norm(alpha * x_ref[...] + mix, lng_ref[...], lnb_ref[...])


def _merge(x2, ya, ybt, g, b_gate, w_a, w_b, w_o, ln_g, ln_b, alpha, seq):
    n_rows = x2.shape[0]
    tiles_per_seq = seq // MERGE_ROWS
    const = lambda i: (0, 0)
    resident = pl.Buffered(1)
    return pl.pallas_call(
        functools.partial(_merge_kernel, alpha=alpha),
        grid=(n_rows // MERGE_ROWS,),
        in_specs=[
            pl.BlockSpec((MERGE_ROWS, D_MODEL), lambda i: (i, 0)),
            pl.BlockSpec((MERGE_ROWS, WIDTH_A), lambda i: (i, 0)),
            pl.BlockSpec((1, MERGE_ROWS // MOBA_BLOCK, WIDTH_B, MOBA_BLOCK),
                         lambda i: (i // tiles_per_seq, i % tiles_per_seq, 0, 0)),
            pl.BlockSpec((MERGE_ROWS, 2 * D_MODEL), lambda i: (i, 0)),
            pl.BlockSpec((1, 2 * D_MODEL), const, pipeline_mode=resident),
            pl.BlockSpec((WIDTH_A, D_MODEL), const, pipeline_mode=resident),
            pl.BlockSpec((WIDTH_B, D_MODEL), const, pipeline_mode=resident),
            pl.BlockSpec((D_MODEL, D_MODEL), const, pipeline_mode=resident),
            pl.BlockSpec((1, D_MODEL), const, pipeline_mode=resident),
            pl.BlockSpec((1, D_MODEL), const, pipeline_mode=resident),
        ],
        out_specs=pl.BlockSpec((MERGE_ROWS, D_MODEL), lambda i: (i, 0)),
        out_shape=jax.ShapeDtypeStruct((n_rows, D_MODEL), F32),
        compiler_params=pltpu.CompilerParams(
            dimension_semantics=("arbitrary",), vmem_limit_bytes=VMEM_LIMIT),
        name="merge_outproj_ln",
    )(x2, ya, ybt, g, b_gate, w_a, w_b, w_o, ln_g, ln_b)


def _ffn_kernel(x_ref, wup_ref, cw_ref, cb_ref, wdn_ref, lng_ref, lnb_ref, o_ref,
                hbuf_ref, carry_ref, acc_ref, *, alpha, tiles_per_seq):
    i = pl.program_id(0)

    @pl.when(i % tiles_per_seq == 0)
    def _():
        carry_ref[...] = jnp.zeros(carry_ref.shape, F32)

    x = x_ref[...]
    xb = x.astype(BF16)
    acc_ref[...] = jnp.zeros(acc_ref.shape, F32)
    pad = carry_ref.shape[0]
    n_steps = D_FF // FF_TILE

    def cols_of(j, half):
        c0 = half * D_FF + j * FF_TILE
        return slice(c0, c0 + FF_TILE)

    def up_proj(j):
        for half in range(2):
            cols = cols_of(j, half)
            hbuf = hbuf_ref.at[(2 * j + half) % hbuf_ref.shape[0]]
            hbuf[0:pad, :] = carry_ref[:, cols]
            h = _dot(xb, wup_ref[:, cols])
            hbuf[pad:, :] = h
            carry_ref[:, cols] = h[FFN_ROWS - pad:, :]

    def conv_act(j):
        conv = []
        for half in range(2):
            cols = cols_of(j, half)
            hbuf = hbuf_ref.at[(2 * j + half) % hbuf_ref.shape[0]]
            cw = cw_ref[:, cols]
            conv.append(cw[0:1, :] * hbuf[pad - 2:pad - 2 + FFN_ROWS, :]
                        + cw[1:2, :] * hbuf[pad - 1:pad - 1 + FFN_ROWS, :]
                        + cw[2:3, :] * hbuf[pad:, :] + cb_ref[:, cols])
        return (_gelu(conv[0]) * conv[1]).astype(BF16)

    for j in range(min(UP_LOOKAHEAD, n_steps)):
        up_proj(j)
    for j in range(n_steps):
        if j + UP_LOOKAHEAD < n_steps:
            up_proj(j + UP_LOOKAHEAD)
        acc_ref[...] += _dot(conv_act(j), wdn_ref[j * FF_TILE:(j + 1) * FF_TILE, :])
    o_ref[...] = _layer_norm(alpha * x + acc_ref[...], lng_ref[...], lnb_ref[...])


def _conv_ffn(x1, w_up, conv_w, conv_b, w_dn, ln_g, ln_b, alpha, seq):
    n_rows = x1.shape[0]
    tiles_per_seq = seq // FFN_ROWS
    const = lambda i: (0, 0)
    resident = pl.Buffered(1)
    pad = 8
    return pl.pallas_call(
        functools.partial(_ffn_kernel, alpha=alpha, tiles_per_seq=tiles_per_seq),
        grid=(n_rows // FFN_ROWS,),
        in_specs=[
            pl.BlockSpec((FFN_ROWS, D_MODEL), lambda i: (i, 0)),
            pl.BlockSpec((D_MODEL, 2 * D_FF), const, pipeline_mode=resident),
            pl.BlockSpec((CONV_WIDTH, 2 * D_FF), const, pipeline_mode=resident),
            pl.BlockSpec((1, 2 * D_FF), const, pipeline_mode=resident),
            pl.BlockSpec((D_FF, D_MODEL), const, pipeline_mode=resident),
            pl.BlockSpec((1, D_MODEL), const, pipeline_mode=resident),
            pl.BlockSpec((1, D_MODEL), const, pipeline_mode=resident),
        ],
        out_specs=pl.BlockSpec((FFN_ROWS, D_MODEL), lambda i: (i, 0)),
        out_shape=jax.ShapeDtypeStruct((n_rows, D_MODEL), F32),
        scratch_shapes=[
            pltpu.VMEM((2 * (UP_LOOKAHEAD + 1), pad + FFN_ROWS, FF_TILE), F32),
            pltpu.VMEM((pad, 2 * D_FF), F32),
            pltpu.VMEM((FFN_ROWS, D_MODEL), F32),
        ],
        compiler_params=pltpu.CompilerParams(
            dimension_semantics=("arbitrary",), vmem_limit_bytes=VMEM_LIMIT),
        name="conv_ffn_ln",
    )(x1, w_up, conv_w, conv_b, w_dn, ln_g, ln_b)


def _rope_tables(seq):
    half = HEAD_DIM_B // 2
    inv_freq = np.float32(ROPE_THETA) ** (-np.arange(half, dtype=np.float32) / np.float32(half))
    ang = np.arange(seq, dtype=np.float32)[:, None] * inv_freq[None, :]
    cos = np.tile(np.cos(ang), (1, LANES // half)).astype(np.float32)
    sin = np.tile(np.sin(ang), (1, LANES // half)).astype(np.float32)
    first_half = (np.arange(LANES) % HEAD_DIM_B) < half
    return jnp.asarray(cos), jnp.asarray(np.where(first_half[None, :], -sin, sin))


def kernel(x, w_in, b_gate, sgu_ln_g, sgu_ln_b, w_spatial, b_spatial, w_branch_a, w_branch_b,
           w_out, ln1_g, ln1_b, w_up, conv_w, conv_b, w_down, ln2_g, ln2_b):
    bsz, seq, d_model = x.shape
    depth = w_in.shape[0]
    assert d_model == D_MODEL and all(seq % t == 0 for t in (PROJ_ROWS, MERGE_ROWS, FFN_ROWS))
    assert all(t % MOBA_BLOCK == 0 for t in (PROJ_ROWS, MERGE_ROWS))
    alpha = float((2.0 * depth) ** 0.25)
    cos_t, sin_t = _rope_tables(seq)
    x2 = x.reshape(bsz * seq, d_model)
    for l in range(depth):
        w_in_b = w_in[l].astype(BF16)
        w_sp = w_spatial[l].astype(BF16).reshape(N_GROUPS_A // 2, 2, CHUNK, CHUNK)
        w_sp = w_sp.transpose(0, 2, 1, 3).reshape(N_GROUPS_A // 2, CHUNK, 2 * CHUNK)
        b_sp = jnp.repeat(b_spatial[l].T, GROUP_DIM_A, axis=1)
        ya, q, k, vext, g = _projection(
            x2, w_in_b, cos_t, sin_t, sgu_ln_g[l][None, :], sgu_ln_b[l][None, :],
            w_sp, b_sp, bsz, seq)
        pen = _moba_gate(q, k, bsz, seq)
        ybt = _moba_attention(q, k, vext, pen, bsz, seq)
        x1 = _merge(x2, ya, ybt, g, b_gate[l][None, :], w_branch_a[l].astype(BF16),
                    w_branch_b[l].astype(BF16), w_out[l].astype(BF16),
                    ln1_g[l][None, :], ln1_b[l][None, :], alpha, seq)
        x2 = _conv_ffn(x1, w_up[l].astype(BF16), conv_w[l], conv_b[l][None, :],
                       w_down[l].astype(BF16), ln2_g[l][None, :], ln2_b[l][None, :], alpha, seq)
    return x2.reshape(bsz, seq, d_model)
```

```python
import functools

import jax
import jax.numpy as jnp
import numpy as np
from jax import lax
from jax.experimental import pallas as pl
from jax.experimental.pallas import tpu as pltpu

F32 = jnp.float32
BF16 = jnp.bfloat16

D_MODEL = 1024
N_GROUPS_A = 8
GROUP_DIM_A = 64
WIDTH_A = N_GROUPS_A * GROUP_DIM_A
CHUNK = 128
N_HEADS_B = 8
HEAD_DIM_B = 64
WIDTH_B = N_HEADS_B * HEAD_DIM_B
MOBA_BLOCK = 256
MOBA_TOPK = 3
ROPE_THETA = 10000.0
D_FF = 2816
CONV_WIDTH = 3
LN_EPS = 1e-5
PROJ_WIDTH = 2 * WIDTH_A + 3 * WIDTH_B + 2 * D_MODEL

LANES = 128
PROJ_ROWS = 1024
MERGE_ROWS = 1024
FF_TILE = 256
FFN_ROWS = 512
UP_LOOKAHEAD = 3
DOWN_GROUP = 4
GATE_CHUNK = 1024
V_EXT_ROWS = HEAD_DIM_B + 16
VMEM_LIMIT = 56 * 1024 * 1024
NEG_INF = float("-inf")
POS_INF = float("inf")
Q_SCALE = np.float32(HEAD_DIM_B ** -0.5 * np.log2(np.e))


def _gelu(x):
    return 0.5 * x * (1.0 + lax.erf(x * np.float32(np.sqrt(0.5))))


def _layer_norm(x, g, b):
    mu = jnp.mean(x, axis=-1, keepdims=True)
    xc = x - mu
    var = jnp.mean(xc * xc, axis=-1, keepdims=True)
    return xc * lax.rsqrt(var + LN_EPS) * g + b


def _dot(a, b):
    return jnp.dot(a, b, preferred_element_type=F32)


def _dot_nt(a, b):
    return lax.dot_general(a, b, (((1,), (1,)), ((), ())), preferred_element_type=F32)


def _dot_tn(a, b):
    return lax.dot_general(a, b, (((0,), (0,)), ((), ())), preferred_element_type=F32)


def _rope(t, cos, sin_signed):
    lane = lax.broadcasted_iota(jnp.int32, (t.shape[0], LANES), 1)
    first_half = (lane % HEAD_DIM_B) < (HEAD_DIM_B // 2)
    out = []
    for c in range(t.shape[1] // LANES):
        blk = t[:, c * LANES:(c + 1) * LANES]
        up = pltpu.roll(blk, LANES - HEAD_DIM_B // 2, 1)
        down = pltpu.roll(blk, HEAD_DIM_B // 2, 1)
        partner = jnp.where(first_half, up, down)
        out.append(blk * cos + partner * sin_signed)
    return jnp.concatenate(out, axis=1)


def _proj_kernel(x_ref, w_ref, cos_ref, sin_ref, lng_ref, lnb_ref, wsp_ref, bsp_ref,
                 ya_ref, q_ref, k_ref, vext_ref, g_ref, wvt_ref):
    @pl.when(pl.program_id(0) == 0)
    def _():
        w_v = w_ref[:, 2 * WIDTH_A + 2 * WIDTH_B:2 * WIDTH_A + 3 * WIDTH_B].astype(F32)
        wvt_ref[...] = w_v.T.astype(BF16)

    xb = x_ref[...].astype(BF16)
    gu = _gelu(_dot(xb, w_ref[:, 0:WIDTH_A]))
    gv = _gelu(_dot(xb, w_ref[:, WIDTH_A:2 * WIDTH_A]))
    g_ref[...] = _dot(xb, w_ref[:, 2 * WIDTH_A + 3 * WIDTH_B:]).astype(BF16)
    cos = cos_ref[...]
    sin = sin_ref[...]
    q = _rope(_dot(xb, w_ref[:, 2 * WIDTH_A:2 * WIDTH_A + WIDTH_B]), cos, sin)
    q_ref[0] = (q * Q_SCALE).astype(BF16)
    k = _rope(_dot(xb, w_ref[:, 2 * WIDTH_A + WIDTH_B:2 * WIDTH_A + 2 * WIDTH_B]), cos, sin)
    k_ref[0] = k.astype(BF16)
    vt = _dot_nt(wvt_ref[...], xb).astype(BF16)
    ones = jnp.ones((V_EXT_ROWS - HEAD_DIM_B, MOBA_BLOCK), BF16)
    for t in range(PROJ_ROWS // MOBA_BLOCK):
        for h in range(N_HEADS_B):
            vext_ref[0, t, h * V_EXT_ROWS:h * V_EXT_ROWS + HEAD_DIM_B, :] = \
                vt[h * HEAD_DIM_B:(h + 1) * HEAD_DIM_B, t * MOBA_BLOCK:(t + 1) * MOBA_BLOCK]
            vext_ref[0, t, h * V_EXT_ROWS + HEAD_DIM_B:(h + 1) * V_EXT_ROWS, :] = ones

    vn = _layer_norm(gv, lng_ref[...], lnb_ref[...]).astype(BF16)
    t_idx = lax.broadcasted_iota(jnp.int32, (CHUNK, 2 * CHUNK), 0)
    s_idx = lax.broadcasted_iota(jnp.int32, (CHUNK, 2 * CHUNK), 1) % CHUNK
    causal = s_idx <= t_idx
    w_pairs = [jnp.where(causal, wsp_ref[p], jnp.zeros((), BF16)) for p in range(N_GROUPS_A // 2)]
    lane = lax.broadcasted_iota(jnp.int32, (CHUNK, LANES), 1)
    left = lane < GROUP_DIM_A
    zero = jnp.zeros((CHUNK, LANES), BF16)
    for c in range(PROJ_ROWS // CHUNK):
        rows = slice(c * CHUNK, (c + 1) * CHUNK)
        mixed = []
        for p in range(N_GROUPS_A // 2):
            blk = vn[rows, p * LANES:(p + 1) * LANES]
            stacked = jnp.concatenate([jnp.where(left, blk, zero), jnp.where(left, zero, blk)], axis=0)
            mixed.append(_dot(w_pairs[p], stacked))
        mixed = jnp.concatenate(mixed, axis=1) + bsp_ref[...]
        ya_ref[rows, :] = (gu[rows, :] * mixed).astype(BF16)


def _projection(x2, w_in_b, cos_t, sin_t, ln_g, ln_b, w_sp, b_sp, bsz, seq):
    n_rows = x2.shape[0]
    tiles_per_seq = seq // PROJ_ROWS
    blocks_per_tile = PROJ_ROWS // MOBA_BLOCK
    const = lambda i: (0, 0)
    resident = pl.Buffered(1)
    seq_tile = lambda i: (i // tiles_per_seq, i % tiles_per_seq, 0)
    return pl.pallas_call(
        _proj_kernel,
        grid=(n_rows // PROJ_ROWS,),
        in_specs=[
            pl.BlockSpec((PROJ_ROWS, D_MODEL), lambda i: (i, 0)),
            pl.BlockSpec((D_MODEL, PROJ_WIDTH), const, pipeline_mode=resident),
            pl.BlockSpec((PROJ_ROWS, LANES), lambda i: (i % tiles_per_seq, 0)),
            pl.BlockSpec((PROJ_ROWS, LANES), lambda i: (i % tiles_per_seq, 0)),
            pl.BlockSpec((1, WIDTH_A), const, pipeline_mode=resident),
            pl.BlockSpec((1, WIDTH_A), const, pipeline_mode=resident),
            pl.BlockSpec((N_GROUPS_A // 2, CHUNK, 2 * CHUNK), lambda i: (0, 0, 0),
                         pipeline_mode=resident),
            pl.BlockSpec((CHUNK, WIDTH_A), const, pipeline_mode=resident),
        ],
        out_specs=[
            pl.BlockSpec((PROJ_ROWS, WIDTH_A), lambda i: (i, 0)),
            pl.BlockSpec((1, PROJ_ROWS, WIDTH_B), seq_tile),
            pl.BlockSpec((1, PROJ_ROWS, WIDTH_B), seq_tile),
            pl.BlockSpec((1, blocks_per_tile, N_HEADS_B * V_EXT_ROWS, MOBA_BLOCK),
                         lambda i: (i // tiles_per_seq, i % tiles_per_seq, 0, 0)),
            pl.BlockSpec((PROJ_ROWS, 2 * D_MODEL), lambda i: (i, 0)),
        ],
        out_shape=[
            jax.ShapeDtypeStruct((n_rows, WIDTH_A), BF16),
            jax.ShapeDtypeStruct((bsz, seq, WIDTH_B), BF16),
            jax.ShapeDtypeStruct((bsz, seq, WIDTH_B), BF16),
            jax.ShapeDtypeStruct((bsz, seq // MOBA_BLOCK, N_HEADS_B * V_EXT_ROWS, MOBA_BLOCK), BF16),
            jax.ShapeDtypeStruct((n_rows, 2 * D_MODEL), BF16),
        ],
        scratch_shapes=[pltpu.VMEM((WIDTH_B, D_MODEL), BF16)],
        compiler_params=pltpu.CompilerParams(
            dimension_semantics=("arbitrary",), vmem_limit_bytes=VMEM_LIMIT),
        name="sgu_moba_projection",
    )(x2, w_in_b, cos_t, sin_t, ln_g, ln_b, w_sp, b_sp)


def _gate_kernel(q_ref, k_ref, pen_ref, *, n_blocks, seq):
    k_mean = jnp.sum(k_ref[0].astype(F32).reshape(n_blocks, MOBA_BLOCK, LANES), axis=1) \
        * np.float32(1.0 / MOBA_BLOCK)
    lane_k = lax.broadcasted_iota(jnp.int32, (n_blocks, LANES), 1)
    blk_row = lax.broadcasted_iota(jnp.int32, (n_blocks, GATE_CHUNK), 0)
    lane_t = lax.broadcasted_iota(jnp.int32, (n_blocks, GATE_CHUNK), 1)
    for h in range(LANES // HEAD_DIM_B):
        km = jnp.where(lane_k // HEAD_DIM_B == h, k_mean, 0.0)
        km_hi = km.astype(BF16)
        km_lo = (km - km_hi.astype(F32)).astype(BF16)
        for c in range(seq // GATE_CHUNK):
            q_c = q_ref[0, c * GATE_CHUNK:(c + 1) * GATE_CHUNK, :]
            gate = _dot_nt(km_hi, q_c) + _dot_nt(km_lo, q_c)
            past = blk_row < (lane_t + c * GATE_CHUNK) // MOBA_BLOCK
            gate = jnp.where(past, gate, NEG_INF)
            sel = jnp.zeros(gate.shape, jnp.bool_)
            for _ in range(MOBA_TOPK):
                best = jnp.max(gate, axis=0, keepdims=True)
                first = jnp.min(jnp.where(gate == best, blk_row, n_blocks), axis=0, keepdims=True)
                pick = blk_row == first
                sel = sel | pick
                gate = jnp.where(pick, NEG_INF, gate)
            pen_ref[0, h, :, c * GATE_CHUNK:(c + 1) * GATE_CHUNK] = \
                jnp.where(sel & past, 0.0, POS_INF).astype(F32)


def _moba_gate(q, k, bsz, seq):
    n_blocks = seq // MOBA_BLOCK
    return pl.pallas_call(
        functools.partial(_gate_kernel, n_blocks=n_blocks, seq=seq),
        grid=(bsz, WIDTH_B // LANES),
        in_specs=[
            pl.BlockSpec((1, seq, LANES), lambda b, p: (b, 0, p)),
            pl.BlockSpec((1, seq, LANES), lambda b, p: (b, 0, p)),
        ],
        out_specs=pl.BlockSpec((1, LANES // HEAD_DIM_B, n_blocks, seq), lambda b, p: (b, p, 0, 0)),
        out_shape=jax.ShapeDtypeStruct((bsz, N_HEADS_B, n_blocks, seq), F32),
        compiler_params=pltpu.CompilerParams(
            dimension_semantics=("arbitrary", "arbitrary"), vmem_limit_bytes=VMEM_LIMIT),
        name="moba_gate",
    )(q, k)


def _attn_kernel(q_ref, k_ref, vext_ref, pen_ref, o_ref, qm_ref, m_ref, acc_ref,
                 sa_ref, samax_ref, sb_ref, sbmax_ref):
    i = pl.program_id(1)
    pair_heads = LANES // HEAD_DIM_B
    n_pairs = N_HEADS_B // pair_heads

    row0 = pl.multiple_of(i * MOBA_BLOCK, MOBA_BLOCK)
    lane_q = lax.broadcasted_iota(jnp.int32, (MOBA_BLOCK, LANES), 1)
    key_pos = lax.broadcasted_iota(jnp.int32, (MOBA_BLOCK, MOBA_BLOCK), 0)
    qry_pos = lax.broadcasted_iota(jnp.int32, (MOBA_BLOCK, MOBA_BLOCK), 1)

    q_pairs = [q_ref[0, :, p * LANES:(p + 1) * LANES] for p in range(n_pairs)]
    k_own = [k_ref[0, pl.ds(row0, MOBA_BLOCK), p * LANES:(p + 1) * LANES] for p in range(n_pairs)]
    v_own = [vext_ref[0, i, h * V_EXT_ROWS:(h + 1) * V_EXT_ROWS, :] for h in range(N_HEADS_B)]
    qm_all, s_all = [], []
    for h in range(N_HEADS_B):
        in_head = lane_q // HEAD_DIM_B == h % pair_heads
        qm = jnp.where(in_head, q_pairs[h // pair_heads], jnp.zeros((), BF16))
        qm_all.append(qm)
        s_all.append(_dot_nt(k_own[h // pair_heads], qm))
    k_first = [k_ref[0, 0:MOBA_BLOCK, p * LANES:(p + 1) * LANES] for p in range(n_pairs)]
    m_all, acc_all, s0_all = [], [], []
    for h in range(N_HEADS_B):
        s0_all.append(_dot_nt(k_first[h // pair_heads], qm_all[h]))
        s = jnp.where(key_pos <= qry_pos, s_all[h], NEG_INF)
        m0 = jnp.max(s, axis=0, keepdims=True)
        m_all.append(m0)
        acc_all.append(_dot(v_own[h], jnp.exp2(s - m0).astype(BF16)))
    for h in range(N_HEADS_B):
        qm_ref[h] = qm_all[h]
        m_ref[h] = m_all[h]
        acc_ref[h] = acc_all[h]
        sa_ref[h] = s0_all[h]
        samax_ref[h] = jnp.max(s0_all[h], axis=0, keepdims=True)

    def load_keys(j):
        col0 = pl.multiple_of(j * MOBA_BLOCK, MOBA_BLOCK)
        return [k_ref[0, pl.ds(col0, MOBA_BLOCK), p * LANES:(p + 1) * LANES] for p in range(n_pairs)]

    def load_block(j):
        v_all = [vext_ref[0, j, h * V_EXT_ROWS:(h + 1) * V_EXT_ROWS, :] for h in range(N_HEADS_B)]
        pen_all = [pen_ref[0, h, pl.ds(j, 1), :] for h in range(N_HEADS_B)]
        return v_all, pen_all

    def absorb_head(h, v_h, pen_h, s_ref, smax_ref):
        m_run = m_ref[h]
        m_new = jnp.maximum(m_run, smax_ref[h] - pen_h)
        p_j = jnp.exp2(s_ref[h] - (m_new + pen_h)).astype(BF16)
        m_ref[h] = m_new
        acc_ref[h] = jnp.exp2(m_run - m_new) * acc_ref[h] + _dot(v_h, p_j)

    def fused(k_next, s_next_ref, smax_next_ref, block, s_ref, smax_ref):
        v_all, pen_all = block
        for h in range(N_HEADS_B):
            s_new = _dot_nt(k_next[h // pair_heads], qm_ref[h])
            s_next_ref[h] = s_new
            smax_next_ref[h] = jnp.max(s_new, axis=0, keepdims=True)
            absorb_head(h, v_all[h], pen_all[h], s_ref, smax_ref)

    def kv_pair(jj, carry):
        j = 2 * jj
        keys_b, keys_a = load_keys(j + 1), load_keys(j + 2)
        block_a, block_b = load_block(j), load_block(j + 1)
        fused(keys_b, sb_ref, sbmax_ref, block_a, sa_ref, samax_ref)
        fused(keys_a, sa_ref, samax_ref, block_b, sb_ref, sbmax_ref)
        return carry

    def kv_quad(jq, carry):
        kv_pair(2 * jq, carry)
        kv_pair(2 * jq + 1, carry)
        return carry

    def kv_oct(jo, carry):
        kv_quad(2 * jo, carry)
        kv_quad(2 * jo + 1, carry)
        return carry

    lax.fori_loop(0, i // 8, kv_oct, 0)
    lax.fori_loop(i // 8 * 2, i // 4, kv_quad, 0)
    lax.fori_loop(i // 4 * 2, i // 2, kv_pair, 0)

    @pl.when(i % 2 == 1)
    def _():
        v_all, pen_all = load_block(i - 1)
        for h in range(N_HEADS_B):
            absorb_head(h, v_all[h], pen_all[h], sa_ref, samax_ref)

    for h in range(N_HEADS_B):
        acc = acc_ref[h]
        out = acc[0:HEAD_DIM_B, :] / acc[HEAD_DIM_B:HEAD_DIM_B + 1, :]
        o_ref[0, 0, h * HEAD_DIM_B:(h + 1) * HEAD_DIM_B, :] = out.astype(BF16)


def _moba_attention(q, k, vext, pen, bsz, seq):
    n_blocks = seq // MOBA_BLOCK
    return pl.pallas_call(
        _attn_kernel,
        grid=(bsz, n_blocks),
        in_specs=[
            pl.BlockSpec((1, MOBA_BLOCK, WIDTH_B), lambda b, i: (b, i, 0)),
            pl.BlockSpec((1, seq, WIDTH_B), lambda b, i: (b, 0, 0)),
            pl.BlockSpec((1, n_blocks, N_HEADS_B * V_EXT_ROWS, MOBA_BLOCK), lambda b, i: (b, 0, 0, 0)),
            pl.BlockSpec((1, N_HEADS_B, n_blocks, MOBA_BLOCK), lambda b, i: (b, 0, 0, i)),
        ],
        out_specs=pl.BlockSpec((1, 1, WIDTH_B, MOBA_BLOCK), lambda b, i: (b, i, 0, 0)),
        out_shape=jax.ShapeDtypeStruct((bsz, n_blocks, WIDTH_B, MOBA_BLOCK), BF16),
        scratch_shapes=[
            pltpu.VMEM((N_HEADS_B, MOBA_BLOCK, LANES), BF16),
            pltpu.VMEM((N_HEADS_B, 1, MOBA_BLOCK), F32),
            pltpu.VMEM((N_HEADS_B, V_EXT_ROWS, MOBA_BLOCK), F32),
            pltpu.VMEM((N_HEADS_B, MOBA_BLOCK, MOBA_BLOCK), F32),
            pltpu.VMEM((N_HEADS_B, 1, MOBA_BLOCK), F32),
            pltpu.VMEM((N_HEADS_B, MOBA_BLOCK, MOBA_BLOCK), F32),
            pltpu.VMEM((N_HEADS_B, 1, MOBA_BLOCK), F32),
        ],
        compiler_params=pltpu.CompilerParams(
            dimension_semantics=("arbitrary", "arbitrary"), vmem_limit_bytes=VMEM_LIMIT),
        name="moba_attention",
    )(q, k, vext, pen)


def _merge_kernel(x_ref, ya_ref, ybt_ref, g_ref, bg_ref, wa_ref, wb_ref, wo_ref, lng_ref, lnb_ref,
                  o_ref, *, alpha):
    def branches(t):
        rows = slice(t * MOBA_BLOCK, (t + 1) * MOBA_BLOCK)
        return (_dot(ya_ref[rows, :], wa_ref[...]),
                _dot_tn(ybt_ref[0, t], wb_ref[...]))

    n_sub = MERGE_ROWS // MOBA_BLOCK
    ab = branches(0)
    for t in range(n_sub):
        ab_next = branches(t + 1) if t + 1 < n_sub else None
        rows = slice(t * MOBA_BLOCK, (t + 1) * MOBA_BLOCK)
        g = g_ref[rows, :].astype(F32) + bg_ref[...]
        merged = (jax.nn.sigmoid(g[:, :D_MODEL]) * ab[0]
                  + jax.nn.sigmoid(g[:, D_MODEL:]) * ab[1])
        mix = _dot(merged.astype(BF16), wo_ref[...])
        o_ref[rows, :] = _layer_norm(alpha * x_ref[rows, :] + mix, lng_ref[...], lnb_ref[...])
        ab = ab_next


def _merge(x2, ya, ybt, g, b_gate, w_a, w_b, w_o, ln_g, ln_b, alpha, seq):
    n_rows = x2.shape[0]
    tiles_per_seq = seq // MERGE_ROWS
    const = lambda i: (0, 0)
    resident = pl.Buffered(1)
    return pl.pallas_call(
        functools.partial(_merge_kernel, alpha=alpha),
        grid=(n_rows // MERGE_ROWS,),
        in_specs=[
            pl.BlockSpec((MERGE_ROWS, D_MODEL), lambda i: (i, 0)),
            pl.BlockSpec((MERGE_ROWS, WIDTH_A), lambda i: (i, 0)),
            pl.BlockSpec((1, MERGE_ROWS // MOBA_BLOCK, WIDTH_B, MOBA_BLOCK),
                         lambda i: (i // tiles_per_seq, i % tiles_per_seq, 0, 0)),
            pl.BlockSpec((MERGE_ROWS, 2 * D_MODEL), lambda i: (i, 0)),
            pl.BlockSpec((1, 2 * D_MODEL), const, pipeline_mode=resident),
            pl.BlockSpec((WIDTH_A, D_MODEL), const, pipeline_mode=resident),
            pl.BlockSpec((WIDTH_B, D_MODEL), const, pipeline_mode=resident),
            pl.BlockSpec((D_MODEL, D_MODEL), const, pipeline_mode=resident),
            pl.BlockSpec((1, D_MODEL), const, pipeline_mode=resident),
            pl.BlockSpec((1, D_MODEL), const, pipeline_mode=resident),
        ],
        out_specs=pl.BlockSpec((MERGE_ROWS, D_MODEL), lambda i: (i, 0)),
        out_shape=jax.ShapeDtypeStruct((n_rows, D_MODEL), F32),
        compiler_params=pltpu.CompilerParams(
            dimension_semantics=("arbitrary",), vmem_limit_bytes=VMEM_LIMIT),
        name="merge_outproj_ln",
    )(x2, ya, ybt, g, b_gate, w_a, w_b, w_o, ln_g, ln_b)


def _ffn_kernel(x_ref, wup_ref, cw_ref, cb_ref, wdn_ref, lng_ref, lnb_ref, o_ref,
                hbuf_ref, carry_ref, act_ref, *, alpha, tiles_per_seq):
    i = pl.program_id(0)

    @pl.when(i % tiles_per_seq == 0)
    def _():
        carry_ref[...] = jnp.zeros(carry_ref.shape, F32)

    x = x_ref[...]
    xb = x.astype(BF16)
    pad = carry_ref.shape[0]
    n_steps = D_FF // FF_TILE

    def cols_of(j, half):
        c0 = half * D_FF + j * FF_TILE
        return slice(c0, c0 + FF_TILE)

    def up_proj(j):
        for half in range(2):
            cols = cols_of(j, half)
            hbuf = hbuf_ref.at[(2 * j + half) % hbuf_ref.shape[0]]
            hbuf[0:pad, :] = carry_ref[:, cols]
            h = _dot(xb, wup_ref[:, cols])
            hbuf[pad:, :] = h
            carry_ref[:, cols] = h[FFN_ROWS - pad:, :]

    def conv_act(j):
        conv = []
        for half in range(2):
            cols = cols_of(j, half)
            hbuf = hbuf_ref.at[(2 * j + half) % hbuf_ref.shape[0]]
            cw = cw_ref[:, cols]
            conv.append(cw[0:1, :] * hbuf[pad - 2:pad - 2 + FFN_ROWS, :]
                        + cw[1:2, :] * hbuf[pad - 1:pad - 1 + FFN_ROWS, :]
                        + cw[2:3, :] * hbuf[pad:, :] + cb_ref[:, cols])
        return (_gelu(conv[0]) * conv[1]).astype(BF16)

    for j in range(min(UP_LOOKAHEAD, n_steps)):
        up_proj(j)
    ffn = None
    for j in range(n_steps):
        if j + UP_LOOKAHEAD < n_steps:
            up_proj(j + UP_LOOKAHEAD)
        act_ref[:, j * FF_TILE:(j + 1) * FF_TILE] = conv_act(j)
        if (j + 1) % DOWN_GROUP == 0 or j + 1 == n_steps:
            k0 = (j // DOWN_GROUP) * DOWN_GROUP * FF_TILE
            part = _dot(act_ref[:, k0:(j + 1) * FF_TILE], wdn_ref[k0:(j + 1) * FF_TILE, :])
            ffn = part if ffn is None else ffn + part
    o_ref[...] = _layer_norm(alpha * x + ffn, lng_ref[...], lnb_ref[...])


def _conv_ffn(x1, w_up, conv_w, conv_b, w_dn, ln_g, ln_b, alpha, seq):
    n_rows = x1.shape[0]
    tiles_per_seq = seq // FFN_ROWS
    const = lambda i: (0, 0)
    resident = pl.Buffered(1)
    pad = 8
    return pl.pallas_call(
        functools.partial(_ffn_kernel, alpha=alpha, tiles_per_seq=tiles_per_seq),
        grid=(n_rows // FFN_ROWS,),
        in_specs=[
            pl.BlockSpec((FFN_ROWS, D_MODEL), lambda i: (i, 0)),
            pl.BlockSpec((D_MODEL, 2 * D_FF), const, pipeline_mode=resident),
            pl.BlockSpec((CONV_WIDTH, 2 * D_FF), const, pipeline_mode=resident),
            pl.BlockSpec((1, 2 * D_FF), const, pipeline_mode=resident),
            pl.BlockSpec((D_FF, D_MODEL), const, pipeline_mode=resident),
            pl.BlockSpec((1, D_MODEL), const, pipeline_mode=resident),
            pl.BlockSpec((1, D_MODEL), const, pipeline_mode=resident),
        ],
        out_specs=pl.BlockSpec((FFN_ROWS, D_MODEL), lambda i: (i, 0)),
        out_shape=jax.ShapeDtypeStruct((n_rows, D_MODEL), F32),
        scratch_shapes=[
            pltpu.VMEM((2 * (UP_LOOKAHEAD + 1), pad + FFN_ROWS, FF_TILE), F32),
            pltpu.VMEM((pad, 2 * D_FF), F32),
            pltpu.VMEM((FFN_ROWS, D_FF), BF16),
        ],
        compiler_params=pltpu.CompilerParams(
            dimension_semantics=("arbitrary",), vmem_limit_bytes=VMEM_LIMIT),
        name="conv_ffn_ln",
    )(x1, w_up, conv_w, conv_b, w_dn, ln_g, ln_b)


def _rope_tables(seq):
    half = HEAD_DIM_B // 2
    inv_freq = np.float32(ROPE_THETA) ** (-np.arange(half, dtype=np.float32) / np.float32(half))
    ang = np.arange(seq, dtype=np.float32)[:, None] * inv_freq[None, :]
    cos = np.tile(np.cos(ang), (1, LANES // half)).astype(np.float32)
    sin = np.tile(np.sin(ang), (1, LANES // half)).astype(np.float32)
    first_half = (np.arange(LANES) % HEAD_DIM_B) < half
    return jnp.asarray(cos), jnp.asarray(np.where(first_half[None, :], -sin, sin))


def kernel(x, w_in, b_gate, sgu_ln_g, sgu_ln_b, w_spatial, b_spatial, w_branch_a, w_branch_b,
           w_out, ln1_g, ln1_b, w_up, conv_w, conv_b, w_down, ln2_g, ln2_b):
    bsz, seq, d_model = x.shape
    depth = w_in.shape[0]
    assert d_model == D_MODEL and all(seq % t == 0 for t in (PROJ_ROWS, MERGE_ROWS, FFN_ROWS))
    assert all(t % MOBA_BLOCK == 0 for t in (PROJ_ROWS, MERGE_ROWS))
    alpha = float((2.0 * depth) ** 0.25)
    cos_t, sin_t = _rope_tables(seq)
    x2 = x.reshape(bsz * seq, d_model)
    for l in range(depth):
        w_in_b = w_in[l].astype(BF16)
        w_sp = w_spatial[l].astype(BF16).reshape(N_GROUPS_A // 2, 2, CHUNK, CHUNK)
        w_sp = w_sp.transpose(0, 2, 1, 3).reshape(N_GROUPS_A // 2, CHUNK, 2 * CHUNK)
        b_sp = jnp.repeat(b_spatial[l].T, GROUP_DIM_A, axis=1)
        ya, q, k, vext, g = _projection(
            x2, w_in_b, cos_t, sin_t, sgu_ln_g[l][None, :], sgu_ln_b[l][None, :],
            w_sp, b_sp, bsz, seq)
        pen = _moba_gate(q, k, bsz, seq)
        ybt = _moba_attention(q, k, vext, pen, bsz, seq)
        x1 = _merge(x2, ya, ybt, g, b_gate[l][None, :], w_branch_a[l].astype(BF16),
                    w_branch_b[l].astype(BF16), w_out[l].astype(BF16),
                    ln1_g[l][None, :], ln1_b[l][None, :], alpha, seq)
        x2 = _conv_ffn(x1, w_up[l].astype(BF16), conv_w[l], conv_b[l][None, :],
                       w_down[l].astype(BF16), ln2_g[l][None, :], ln2_b[l][None, :], alpha, seq)
    return x2.reshape(bsz, seq, d_model)
```

```python
import functools

import jax
import jax.numpy as jnp
import numpy as np
from jax import lax
from jax.experimental import pallas as pl
from jax.experimental.pallas import tpu as pltpu

F32 = jnp.float32
BF16 = jnp.bfloat16

D_MODEL = 1024
N_GROUPS_A = 8
GROUP_DIM_A = 64
WIDTH_A = N_GROUPS_A * GROUP_DIM_A
CHUNK = 128
N_HEADS_B = 8
HEAD_DIM_B = 64
WIDTH_B = N_HEADS_B * HEAD_DIM_B
MOBA_BLOCK = 256
MOBA_TOPK = 3
ROPE_THETA = 10000.0
D_FF = 2816
CONV_WIDTH = 3
LN_EPS = 1e-5
PROJ_WIDTH = 2 * WIDTH_A + 3 * WIDTH_B + 2 * D_MODEL

LANES = 128
PROJ_ROWS = 1024
MERGE_ROWS = 1024
FF_TILE = 256
FFN_ROWS = 512
UP_LOOKAHEAD = 3
GATE_CHUNK = 1024
V_EXT_ROWS = HEAD_DIM_B + 16
VMEM_LIMIT = 56 * 1024 * 1024
NEG_INF = float("-inf")
POS_INF = float("inf")
M_INIT = float(np.finfo(np.float32).min)
Q_SCALE = np.float32(HEAD_DIM_B ** -0.5 * np.log2(np.e))


def _gelu(x):
    return 0.5 * x * (1.0 + lax.erf(x * np.float32(np.sqrt(0.5))))


def _layer_norm(x, g, b):
    mu = jnp.mean(x, axis=-1, keepdims=True)
    xc = x - mu
    var = jnp.mean(xc * xc, axis=-1, keepdims=True)
    return xc * lax.rsqrt(var + LN_EPS) * g + b


def _dot(a, b):
    return jnp.dot(a, b, preferred_element_type=F32)


def _dot_nt(a, b):
    return lax.dot_general(a, b, (((1,), (1,)), ((), ())), preferred_element_type=F32)


def _dot_tn(a, b):
    return lax.dot_general(a, b, (((0,), (0,)), ((), ())), preferred_element_type=F32)


def _rope(t, cos, sin_signed):
    lane = lax.broadcasted_iota(jnp.int32, (t.shape[0], LANES), 1)
    first_half = (lane % HEAD_DIM_B) < (HEAD_DIM_B // 2)
    out = []
    for c in range(t.shape[1] // LANES):
        blk = t[:, c * LANES:(c + 1) * LANES]
        up = pltpu.roll(blk, LANES - HEAD_DIM_B // 2, 1)
        down = pltpu.roll(blk, HEAD_DIM_B // 2, 1)
        partner = jnp.where(first_half, up, down)
        out.append(blk * cos + partner * sin_signed)
    return jnp.concatenate(out, axis=1)


def _proj_kernel(x_ref, w_ref, cos_ref, sin_ref, lng_ref, lnb_ref, wsp_ref, bsp_ref,
                 ya_ref, q_ref, k_ref, vext_ref, g_ref, wvt_ref):
    @pl.when(pl.program_id(0) == 0)
    def _():
        w_v = w_ref[:, 2 * WIDTH_A + 2 * WIDTH_B:2 * WIDTH_A + 3 * WIDTH_B].astype(F32)
        wvt_ref[...] = w_v.T.astype(BF16)

    xb = x_ref[...].astype(BF16)
    gu = _gelu(_dot(xb, w_ref[:, 0:WIDTH_A]))
    gv = _gelu(_dot(xb, w_ref[:, WIDTH_A:2 * WIDTH_A]))
    g_ref[...] = _dot(xb, w_ref[:, 2 * WIDTH_A + 3 * WIDTH_B:]).astype(BF16)
    cos = cos_ref[...]
    sin = sin_ref[...]
    q = _rope(_dot(xb, w_ref[:, 2 * WIDTH_A:2 * WIDTH_A + WIDTH_B]), cos, sin)
    q_ref[0] = (q * Q_SCALE).astype(BF16)
    k = _rope(_dot(xb, w_ref[:, 2 * WIDTH_A + WIDTH_B:2 * WIDTH_A + 2 * WIDTH_B]), cos, sin)
    k_ref[0] = k.astype(BF16)
    vt = _dot_nt(wvt_ref[...], xb).astype(BF16)
    ones = jnp.ones((V_EXT_ROWS - HEAD_DIM_B, MOBA_BLOCK), BF16)
    for t in range(PROJ_ROWS // MOBA_BLOCK):
        for h in range(N_HEADS_B):
            vext_ref[0, t, h * V_EXT_ROWS:h * V_EXT_ROWS + HEAD_DIM_B, :] = \
                vt[h * HEAD_DIM_B:(h + 1) * HEAD_DIM_B, t * MOBA_BLOCK:(t + 1) * MOBA_BLOCK]
            vext_ref[0, t, h * V_EXT_ROWS + HEAD_DIM_B:(h + 1) * V_EXT_ROWS, :] = ones

    vn = _layer_norm(gv, lng_ref[...], lnb_ref[...]).astype(BF16)
    t_idx = lax.broadcasted_iota(jnp.int32, (CHUNK, 2 * CHUNK), 0)
    s_idx = lax.broadcasted_iota(jnp.int32, (CHUNK, 2 * CHUNK), 1) % CHUNK
    causal = s_idx <= t_idx
    w_pairs = [jnp.where(causal, wsp_ref[p], jnp.zeros((), BF16)) for p in range(N_GROUPS_A // 2)]
    lane = lax.broadcasted_iota(jnp.int32, (CHUNK, LANES), 1)
    left = lane < GROUP_DIM_A
    zero = jnp.zeros((CHUNK, LANES), BF16)
    for c in range(PROJ_ROWS // CHUNK):
        rows = slice(c * CHUNK, (c + 1) * CHUNK)
        mixed = []
        for p in range(N_GROUPS_A // 2):
            blk = vn[rows, p * LANES:(p + 1) * LANES]
            stacked = jnp.concatenate([jnp.where(left, blk, zero), jnp.where(left, zero, blk)], axis=0)
            mixed.append(_dot(w_pairs[p], stacked))
        mixed = jnp.concatenate(mixed, axis=1) + bsp_ref[...]
        ya_ref[rows, :] = (gu[rows, :] * mixed).astype(BF16)


def _projection(x2, w_in_b, cos_t, sin_t, ln_g, ln_b, w_sp, b_sp, bsz, seq):
    n_rows = x2.shape[0]
    tiles_per_seq = seq // PROJ_ROWS
    blocks_per_tile = PROJ_ROWS // MOBA_BLOCK
    const = lambda i: (0, 0)
    resident = pl.Buffered(1)
    seq_tile = lambda i: (i // tiles_per_seq, i % tiles_per_seq, 0)
    return pl.pallas_call(
        _proj_kernel,
        grid=(n_rows // PROJ_ROWS,),
        in_specs=[
            pl.BlockSpec((PROJ_ROWS, D_MODEL), lambda i: (i, 0)),
            pl.BlockSpec((D_MODEL, PROJ_WIDTH), const, pipeline_mode=resident),
            pl.BlockSpec((PROJ_ROWS, LANES), lambda i: (i % tiles_per_seq, 0)),
            pl.BlockSpec((PROJ_ROWS, LANES), lambda i: (i % tiles_per_seq, 0)),
            pl.BlockSpec((1, WIDTH_A), const, pipeline_mode=resident),
            pl.BlockSpec((1, WIDTH_A), const, pipeline_mode=resident),
            pl.BlockSpec((N_GROUPS_A // 2, CHUNK, 2 * CHUNK), lambda i: (0, 0, 0),
                         pipeline_mode=resident),
            pl.BlockSpec((CHUNK, WIDTH_A), const, pipeline_mode=resident),
        ],
        out_specs=[
            pl.BlockSpec((PROJ_ROWS, WIDTH_A), lambda i: (i, 0)),
            pl.BlockSpec((1, PROJ_ROWS, WIDTH_B), seq_tile),
            pl.BlockSpec((1, PROJ_ROWS, WIDTH_B), seq_tile),
            pl.BlockSpec((1, blocks_per_tile, N_HEADS_B * V_EXT_ROWS, MOBA_BLOCK),
                         lambda i: (i // tiles_per_seq, i % tiles_per_seq, 0, 0)),
            pl.BlockSpec((PROJ_ROWS, 2 * D_MODEL), lambda i: (i, 0)),
        ],
        out_shape=[
            jax.ShapeDtypeStruct((n_rows, WIDTH_A), BF16),
            jax.ShapeDtypeStruct((bsz, seq, WIDTH_B), BF16),
            jax.ShapeDtypeStruct((bsz, seq, WIDTH_B), BF16),
            jax.ShapeDtypeStruct((bsz, seq // MOBA_BLOCK, N_HEADS_B * V_EXT_ROWS, MOBA_BLOCK), BF16),
            jax.ShapeDtypeStruct((n_rows, 2 * D_MODEL), BF16),
        ],
        scratch_shapes=[pltpu.VMEM((WIDTH_B, D_MODEL), BF16)],
        compiler_params=pltpu.CompilerParams(
            dimension_semantics=("arbitrary",), vmem_limit_bytes=VMEM_LIMIT),
        name="sgu_moba_projection",
    )(x2, w_in_b, cos_t, sin_t, ln_g, ln_b, w_sp, b_sp)


def _gate_kernel(q_ref, k_ref, pen_ref, *, n_blocks, seq):
    k_mean = jnp.sum(k_ref[0].astype(F32).reshape(n_blocks, MOBA_BLOCK, LANES), axis=1) \
        * np.float32(1.0 / MOBA_BLOCK)
    lane_k = lax.broadcasted_iota(jnp.int32, (n_blocks, LANES), 1)
    blk_row = lax.broadcasted_iota(jnp.int32, (n_blocks, GATE_CHUNK), 0)
    lane_t = lax.broadcasted_iota(jnp.int32, (n_blocks, GATE_CHUNK), 1)
    for h in range(LANES // HEAD_DIM_B):
        km = jnp.where(lane_k // HEAD_DIM_B == h, k_mean, 0.0)
        km_hi = km.astype(BF16)
        km_lo = (km - km_hi.astype(F32)).astype(BF16)
        for c in range(seq // GATE_CHUNK):
            q_c = q_ref[0, c * GATE_CHUNK:(c + 1) * GATE_CHUNK, :]
            gate = _dot_nt(km_hi, q_c) + _dot_nt(km_lo, q_c)
            past = blk_row < (lane_t + c * GATE_CHUNK) // MOBA_BLOCK
            gate = jnp.where(past, gate, NEG_INF)
            sel = jnp.zeros(gate.shape, jnp.bool_)
            for _ in range(MOBA_TOPK):
                best = jnp.max(gate, axis=0, keepdims=True)
                first = jnp.min(jnp.where(gate == best, blk_row, n_blocks), axis=0, keepdims=True)
                pick = blk_row == first
                sel = sel | pick
                gate = jnp.where(pick, NEG_INF, gate)
            pen_ref[0, h, :, c * GATE_CHUNK:(c + 1) * GATE_CHUNK] = \
                jnp.where(sel & past, 0.0, POS_INF).astype(F32)


def _moba_gate(q, k, bsz, seq):
    n_blocks = seq // MOBA_BLOCK
    return pl.pallas_call(
        functools.partial(_gate_kernel, n_blocks=n_blocks, seq=seq),
        grid=(bsz, WIDTH_B // LANES),
        in_specs=[
            pl.BlockSpec((1, seq, LANES), lambda b, p: (b, 0, p)),
            pl.BlockSpec((1, seq, LANES), lambda b, p: (b, 0, p)),
        ],
        out_specs=pl.BlockSpec((1, LANES // HEAD_DIM_B, n_blocks, seq), lambda b, p: (b, p, 0, 0)),
        out_shape=jax.ShapeDtypeStruct((bsz, N_HEADS_B, n_blocks, seq), F32),
        compiler_params=pltpu.CompilerParams(
            dimension_semantics=("arbitrary", "arbitrary"), vmem_limit_bytes=VMEM_LIMIT),
        name="moba_gate",
    )(q, k)


def _attn_kernel(q_ref, k_ref, vext_ref, pen_ref, o_ref, qm_ref, m_ref, acc_ref,
                 sa_ref, samax_ref, sb_ref, sbmax_ref):
    i = pl.program_id(1)
    pair_heads = LANES // HEAD_DIM_B
    n_pairs = N_HEADS_B // pair_heads

    lane_q = lax.broadcasted_iota(jnp.int32, (MOBA_BLOCK, LANES), 1)
    key_pos = lax.broadcasted_iota(jnp.int32, (MOBA_BLOCK, MOBA_BLOCK), 0)
    qry_pos = lax.broadcasted_iota(jnp.int32, (MOBA_BLOCK, MOBA_BLOCK), 1)

    q_pairs = [q_ref[0, :, p * LANES:(p + 1) * LANES] for p in range(n_pairs)]
    k_first = [k_ref[0, 0:MOBA_BLOCK, p * LANES:(p + 1) * LANES] for p in range(n_pairs)]
    for h in range(N_HEADS_B):
        in_head = lane_q // HEAD_DIM_B == h % pair_heads
        qm = jnp.where(in_head, q_pairs[h // pair_heads], jnp.zeros((), BF16))
        s0 = _dot_nt(k_first[h // pair_heads], qm)
        qm_ref[h] = qm
        sa_ref[h] = s0
        samax_ref[h] = jnp.max(s0, axis=0, keepdims=True)
        m_ref[h] = jnp.full((1, MOBA_BLOCK), M_INIT, F32)
        acc_ref[h] = jnp.zeros((V_EXT_ROWS, MOBA_BLOCK), F32)

    def load_keys(j):
        col0 = pl.multiple_of(j * MOBA_BLOCK, MOBA_BLOCK)
        return [k_ref[0, pl.ds(col0, MOBA_BLOCK), p * LANES:(p + 1) * LANES] for p in range(n_pairs)]

    def load_block(j):
        v_all = [vext_ref[0, j, h * V_EXT_ROWS:(h + 1) * V_EXT_ROWS, :] for h in range(N_HEADS_B)]
        pen_all = [pen_ref[0, h, pl.ds(j, 1), :] for h in range(N_HEADS_B)]
        return v_all, pen_all

    def absorb_head(h, v_h, pen_h, s_ref, smax_ref):
        m_run = m_ref[h]
        m_new = jnp.maximum(m_run, smax_ref[h] - pen_h)
        p_j = jnp.exp2(s_ref[h] - (m_new + pen_h)).astype(BF16)
        m_ref[h] = m_new
        acc_ref[h] = jnp.exp2(m_run - m_new) * acc_ref[h] + _dot(v_h, p_j)

    def fused(k_next, s_next_ref, smax_next_ref, block, s_ref, smax_ref):
        v_all, pen_all = block
        for h in range(N_HEADS_B):
            s_new = _dot_nt(k_next[h // pair_heads], qm_ref[h])
            s_next_ref[h] = s_new
            smax_next_ref[h] = jnp.max(s_new, axis=0, keepdims=True)
            absorb_head(h, v_all[h], pen_all[h], s_ref, smax_ref)

    def absorb_own(s_ref):
        v_own = [vext_ref[0, i, h * V_EXT_ROWS:(h + 1) * V_EXT_ROWS, :] for h in range(N_HEADS_B)]
        for h in range(N_HEADS_B):
            s = jnp.where(key_pos <= qry_pos, s_ref[h], NEG_INF)
            m_run = m_ref[h]
            m_new = jnp.maximum(m_run, jnp.max(s, axis=0, keepdims=True))
            p_own = jnp.exp2(s - m_new).astype(BF16)
            acc = jnp.exp2(m_run - m_new) * acc_ref[h] + _dot(v_own[h], p_own)
            out = acc[0:HEAD_DIM_B, :] / acc[HEAD_DIM_B:HEAD_DIM_B + 1, :]
            o_ref[0, 0, h * HEAD_DIM_B:(h + 1) * HEAD_DIM_B, :] = out.astype(BF16)

    def kv_pair(jj, carry):
        j = 2 * jj
        keys_b, keys_a = load_keys(j + 1), load_keys(j + 2)
        block_a, block_b = load_block(j), load_block(j + 1)
        fused(keys_b, sb_ref, sbmax_ref, block_a, sa_ref, samax_ref)
        fused(keys_a, sa_ref, samax_ref, block_b, sb_ref, sbmax_ref)
        return carry

    def kv_quad(jq, carry):
        kv_pair(2 * jq, carry)
        kv_pair(2 * jq + 1, carry)
        return carry

    def kv_oct(jo, carry):
        kv_quad(2 * jo, carry)
        kv_quad(2 * jo + 1, carry)
        return carry

    lax.fori_loop(0, i // 8, kv_oct, 0)
    lax.fori_loop(i // 8 * 2, i // 4, kv_quad, 0)
    lax.fori_loop(i // 4 * 2, i // 2, kv_pair, 0)

    @pl.when(i % 2 == 0)
    def _():
        absorb_own(sa_ref)

    @pl.when(i % 2 == 1)
    def _():
        fused(load_keys(i), sb_ref, sbmax_ref, load_block(i - 1), sa_ref, samax_ref)
        absorb_own(sb_ref)


def _moba_attention(q, k, vext, pen, bsz, seq):
    n_blocks = seq // MOBA_BLOCK
    return pl.pallas_call(
        _attn_kernel,
        grid=(bsz, n_blocks),
        in_specs=[
            pl.BlockSpec((1, MOBA_BLOCK, WIDTH_B), lambda b, i: (b, i, 0)),
            pl.BlockSpec((1, seq, WIDTH_B), lambda b, i: (b, 0, 0)),
            pl.BlockSpec((1, n_blocks, N_HEADS_B * V_EXT_ROWS, MOBA_BLOCK), lambda b, i: (b, 0, 0, 0)),
            pl.BlockSpec((1, N_HEADS_B, n_blocks, MOBA_BLOCK), lambda b, i: (b, 0, 0, i)),
        ],
        out_specs=pl.BlockSpec((1, 1, WIDTH_B, MOBA_BLOCK), lambda b, i: (b, i, 0, 0)),
        out_shape=jax.ShapeDtypeStruct((bsz, n_blocks, WIDTH_B, MOBA_BLOCK), BF16),
        scratch_shapes=[
            pltpu.VMEM((N_HEADS_B, MOBA_BLOCK, LANES), BF16),
            pltpu.VMEM((N_HEADS_B, 1, MOBA_BLOCK), F32),
            pltpu.VMEM((N_HEADS_B, V_EXT_ROWS, MOBA_BLOCK), F32),
            pltpu.VMEM((N_HEADS_B, MOBA_BLOCK, MOBA_BLOCK), F32),
            pltpu.VMEM((N_HEADS_B, 1, MOBA_BLOCK), F32),
            pltpu.VMEM((N_HEADS_B, MOBA_BLOCK, MOBA_BLOCK), F32),
            pltpu.VMEM((N_HEADS_B, 1, MOBA_BLOCK), F32),
        ],
        compiler_params=pltpu.CompilerParams(
            dimension_semantics=("arbitrary", "arbitrary"), vmem_limit_bytes=VMEM_LIMIT),
        name="moba_attention",
    )(q, k, vext, pen)


def _merge_kernel(x_ref, ya_ref, ybt_ref, g_ref, bg_ref, wa_ref, wb_ref, wo_ref, lng_ref, lnb_ref,
                  o_ref, *, alpha):
    branch_a = _dot(ya_ref[...], wa_ref[...])
    branch_b = jnp.concatenate(
        [_dot_tn(ybt_ref[0, t], wb_ref[...]) for t in range(MERGE_ROWS // MOBA_BLOCK)], axis=0)
    g = g_ref[...].astype(F32) + bg_ref[...]
    merged = (jax.nn.sigmoid(g[:, :D_MODEL]) * branch_a
              + jax.nn.sigmoid(g[:, D_MODEL:]) * branch_b)
    mix = _dot(merged.astype(BF16), wo_ref[...])
    o_ref[...] = _layer_norm(alpha * x_ref[...] + mix, lng_ref[...], lnb_ref[...])


def _merge(x2, ya, ybt, g, b_gate, w_a, w_b, w_o, ln_g, ln_b, alpha, seq):
    n_rows = x2.shape[0]
    tiles_per_seq = seq // MERGE_ROWS
    const = lambda i: (0, 0)
    resident = pl.Buffered(1)
    return pl.pallas_call(
        functools.partial(_merge_kernel, alpha=alpha),
        grid=(n_rows // MERGE_ROWS,),
        in_specs=[
            pl.BlockSpec((MERGE_ROWS, D_MODEL), lambda i: (i, 0)),
            pl.BlockSpec((MERGE_ROWS, WIDTH_A), lambda i: (i, 0)),
            pl.BlockSpec((1, MERGE_ROWS // MOBA_BLOCK, WIDTH_B, MOBA_BLOCK),
                         lambda i: (i // tiles_per_seq, i % tiles_per_seq, 0, 0)),
            pl.BlockSpec((MERGE_ROWS, 2 * D_MODEL), lambda i: (i, 0)),
            pl.BlockSpec((1, 2 * D_MODEL), const, pipeline_mode=resident),
            pl.BlockSpec((WIDTH_A, D_MODEL), const, pipeline_mode=resident),
            pl.BlockSpec((WIDTH_B, D_MODEL), const, pipeline_mode=resident),
            pl.BlockSpec((D_MODEL, D_MODEL), const, pipeline_mode=resident),
            pl.BlockSpec((1, D_MODEL), const, pipeline_mode=resident),
            pl.BlockSpec((1, D_MODEL), const, pipeline_mode=resident),
        ],
        out_specs=pl.BlockSpec((MERGE_ROWS, D_MODEL), lambda i: (i, 0)),
        out_shape=jax.ShapeDtypeStruct((n_rows, D_MODEL), F32),
        compiler_params=pltpu.CompilerParams(
            dimension_semantics=("arbitrary",), vmem_limit_bytes=VMEM_LIMIT),
        name="merge_outproj_ln",
    )(x2, ya, ybt, g, b_gate, w_a, w_b, w_o, ln_g, ln_b)


def _ffn_kernel(x_ref, wup_ref, cw_ref, cb_ref, wdn_ref, lng_ref, lnb_ref, o_ref,
                hbuf_ref, carry_ref, acc_ref, *, alpha, tiles_per_seq):
    i = pl.program_id(0)

    @pl.when(i % tiles_per_seq == 0)
    def _():
        carry_ref[...] = jnp.zeros(carry_ref.shape, F32)

    x = x_ref[...]
    xb = x.astype(BF16)
    acc_ref[...] = jnp.zeros(acc_ref.shape, F32)
    pad = carry_ref.shape[0]
    n_steps = D_FF // FF_TILE

    def cols_of(j, half):
        c0 = half * D_FF + j * FF_TILE
        return slice(c0, c0 + FF_TILE)

    def up_proj(j):
        for half in range(2):
            cols = cols_of(j, half)
            hbuf = hbuf_ref.at[(2 * j + half) % hbuf_ref.shape[0]]
            hbuf[0:pad, :] = carry_ref[:, cols]
            h = _dot(xb, wup_ref[:, cols])
            hbuf[pad:, :] = h
            carry_ref[:, cols] = h[FFN_ROWS - pad:, :]

    def conv_act(j):
        conv = []
        for half in range(2):
            cols = cols_of(j, half)
            hbuf = hbuf_ref.at[(2 * j + half) % hbuf_ref.shape[0]]
            cw = cw_ref[:, cols]
            conv.append(cw[0:1, :] * hbuf[pad - 2:pad - 2 + FFN_ROWS, :]
                        + cw[1:2, :] * hbuf[pad - 1:pad - 1 + FFN_ROWS, :]
                        + cw[2:3, :] * hbuf[pad:, :] + cb_ref[:, cols])
        return (_gelu(conv[0]) * conv[1]).astype(BF16)

    for j in range(min(UP_LOOKAHEAD, n_steps)):
        up_proj(j)
    for j in range(n_steps):
        if j + UP_LOOKAHEAD < n_steps:
            up_proj(j + UP_LOOKAHEAD)
        acc_ref[...] += _dot(conv_act(j), wdn_ref[j * FF_TILE:(j + 1) * FF_TILE, :])
    o_ref[...] = _layer_norm(alpha * x + acc_ref[...], lng_ref[...], lnb_ref[...])


def _conv_ffn(x1, w_up, conv_w, conv_b, w_dn, ln_g, ln_b, alpha, seq):
    n_rows = x1.shape[0]
    tiles_per_seq = seq // FFN_ROWS
    const = lambda i: (0, 0)
    resident = pl.Buffered(1)
    pad = 8
    return pl.pallas_call(
        functools.partial(_ffn_kernel, alpha=alpha, tiles_per_seq=tiles_per_seq),
        grid=(n_rows // FFN_ROWS,),
        in_specs=[
            pl.BlockSpec((FFN_ROWS, D_MODEL), lambda i: (i, 0)),
            pl.BlockSpec((D_MODEL, 2 * D_FF), const, pipeline_mode=resident),
            pl.BlockSpec((CONV_WIDTH, 2 * D_FF), const, pipeline_mode=resident),
            pl.BlockSpec((1, 2 * D_FF), const, pipeline_mode=resident),
            pl.BlockSpec((D_FF, D_MODEL), const, pipeline_mode=resident),
            pl.BlockSpec((1, D_MODEL), const, pipeline_mode=resident),
            pl.BlockSpec((1, D_MODEL), const, pipeline_mode=resident),
        ],
        out_specs=pl.BlockSpec((FFN_ROWS, D_MODEL), lambda i: (i, 0)),
        out_shape=jax.ShapeDtypeStruct((n_rows, D_MODEL), F32),
        scratch_shapes=[
            pltpu.VMEM((2 * (UP_LOOKAHEAD + 1), pad + FFN_ROWS, FF_TILE), F32),
            pltpu.VMEM((pad, 2 * D_FF), F32),
            pltpu.VMEM((FFN_ROWS, D_MODEL), F32),
        ],
        compiler_params=pltpu.CompilerParams(
            dimension_semantics=("arbitrary",), vmem_limit_bytes=VMEM_LIMIT),
        name="conv_ffn_ln",
    )(x1, w_up, conv_w, conv_b, w_dn, ln_g, ln_b)


def _rope_tables(seq):
    half = HEAD_DIM_B // 2
    inv_freq = np.float32(ROPE_THETA) ** (-np.arange(half, dtype=np.float32) / np.float32(half))
    ang = np.arange(seq, dtype=np.float32)[:, None] * inv_freq[None, :]
    cos = np.tile(np.cos(ang), (1, LANES // half)).astype(np.float32)
    sin = np.tile(np.sin(ang), (1, LANES // half)).astype(np.float32)
    first_half = (np.arange(LANES) % HEAD_DIM_B) < half
    return jnp.asarray(cos), jnp.asarray(np.where(first_half[None, :], -sin, sin))


def kernel(x, w_in, b_gate, sgu_ln_g, sgu_ln_b, w_spatial, b_spatial, w_branch_a, w_branch_b,
           w_out, ln1_g, ln1_b, w_up, conv_w, conv_b, w_down, ln2_g, ln2_b):
    bsz, seq, d_model = x.shape
    depth = w_in.shape[0]
    assert d_model == D_MODEL and all(seq % t == 0 for t in (PROJ_ROWS, MERGE_ROWS, FFN_ROWS))
    assert all(t % MOBA_BLOCK == 0 for t in (PROJ_ROWS, MERGE_ROWS))
    alpha = float((2.0 * depth) ** 0.25)
    cos_t, sin_t = _rope_tables(seq)
    x2 = x.reshape(bsz * seq, d_model)
    for l in range(depth):
        w_in_b = w_in[l].astype(BF16)
        w_sp = w_spatial[l].astype(BF16).reshape(N_GROUPS_A // 2, 2, CHUNK, CHUNK)
        w_sp = w_sp.transpose(0, 2, 1, 3).reshape(N_GROUPS_A // 2, CHUNK, 2 * CHUNK)
        b_sp = jnp.repeat(b_spatial[l].T, GROUP_DIM_A, axis=1)
        ya, q, k, vext, g = _projection(
            x2, w_in_b, cos_t, sin_t, sgu_ln_g[l][None, :], sgu_ln_b[l][None, :],
            w_sp, b_sp, bsz, seq)
        pen = _moba_gate(q, k, bsz, seq)
        ybt = _moba_attention(q, k, vext, pen, bsz, seq)
        x1 = _merge(x2, ya, ybt, g, b_gate[l][None, :], w_branch_a[l].astype(BF16),
                    w_branch_b[l].astype(BF16), w_out[l].astype(BF16),
                    ln1_g[l][None, :], ln1_b[l][None, :], alpha, seq)
        x2 = _conv_ffn(x1, w_up[l].astype(BF16), conv_w[l], conv_b[l][None, :],
                       w_down[l].astype(BF16), ln2_g[l][None, :], ln2_b[l][None, :], alpha, seq)
    return x2.reshape(bsz, seq, d_model)
```

```python
import functools

import jax
import jax.numpy as jnp
import numpy as np
from jax import lax
from jax.experimental import pallas as pl
from jax.experimental.pallas import tpu as pltpu

F32 = jnp.float32
BF16 = jnp.bfloat16

D_MODEL = 1024
N_GROUPS_A = 8
GROUP_DIM_A = 64
WIDTH_A = N_GROUPS_A * GROUP_DIM_A
CHUNK = 128
N_HEADS_B = 8
HEAD_DIM_B = 64
WIDTH_B = N_HEADS_B * HEAD_DIM_B
MOBA_BLOCK = 256
MOBA_TOPK = 3
ROPE_THETA = 10000.0
D_FF = 2816
CONV_WIDTH = 3
LN_EPS = 1e-5
PROJ_WIDTH = 2 * WIDTH_A + 3 * WIDTH_B + 2 * D_MODEL

LANES = 128
PROJ_ROWS = 1024
MERGE_ROWS = 1024
FF_TILE = 256
FFN_ROWS = 512
UP_LOOKAHEAD = 3
GATE_CHUNK = 1024
V_EXT_ROWS = HEAD_DIM_B + 16
VMEM_LIMIT = 56 * 1024 * 1024
NEG_INF = float("-inf")
POS_INF = float("inf")
M_INIT = float(np.finfo(np.float32).min)
ATTN_Q = 2 * MOBA_BLOCK
Q_SCALE = np.float32(HEAD_DIM_B ** -0.5 * np.log2(np.e))


def _gelu(x):
    return 0.5 * x * (1.0 + lax.erf(x * np.float32(np.sqrt(0.5))))


def _layer_norm(x, g, b):
    mu = jnp.mean(x, axis=-1, keepdims=True)
    xc = x - mu
    var = jnp.mean(xc * xc, axis=-1, keepdims=True)
    return xc * lax.rsqrt(var + LN_EPS) * g + b


def _dot(a, b):
    return jnp.dot(a, b, preferred_element_type=F32)


def _dot_nt(a, b):
    return lax.dot_general(a, b, (((1,), (1,)), ((), ())), preferred_element_type=F32)


def _dot_tn(a, b):
    return lax.dot_general(a, b, (((0,), (0,)), ((), ())), preferred_element_type=F32)


def _rope(t, cos, sin_signed):
    lane = lax.broadcasted_iota(jnp.int32, (t.shape[0], LANES), 1)
    first_half = (lane % HEAD_DIM_B) < (HEAD_DIM_B // 2)
    out = []
    for c in range(t.shape[1] // LANES):
        blk = t[:, c * LANES:(c + 1) * LANES]
        up = pltpu.roll(blk, LANES - HEAD_DIM_B // 2, 1)
        down = pltpu.roll(blk, HEAD_DIM_B // 2, 1)
        partner = jnp.where(first_half, up, down)
        out.append(blk * cos + partner * sin_signed)
    return jnp.concatenate(out, axis=1)


def _proj_kernel(x_ref, w_ref, cos_ref, sin_ref, lng_ref, lnb_ref, wsp_ref, bsp_ref,
                 ya_ref, q_ref, k_ref, vext_ref, g_ref, wvt_ref):
    @pl.when(pl.program_id(0) == 0)
    def _():
        w_v = w_ref[:, 2 * WIDTH_A + 2 * WIDTH_B:2 * WIDTH_A + 3 * WIDTH_B].astype(F32)
        wvt_ref[...] = w_v.T.astype(BF16)

    xb = x_ref[...].astype(BF16)
    gu = _gelu(_dot(xb, w_ref[:, 0:WIDTH_A]))
    gv = _gelu(_dot(xb, w_ref[:, WIDTH_A:2 * WIDTH_A]))
    g_ref[...] = _dot(xb, w_ref[:, 2 * WIDTH_A + 3 * WIDTH_B:]).astype(BF16)
    cos = cos_ref[...]
    sin = sin_ref[...]
    q = _rope(_dot(xb, w_ref[:, 2 * WIDTH_A:2 * WIDTH_A + WIDTH_B]), cos, sin)
    q_ref[0] = (q * Q_SCALE).astype(BF16)
    k = _rope(_dot(xb, w_ref[:, 2 * WIDTH_A + WIDTH_B:2 * WIDTH_A + 2 * WIDTH_B]), cos, sin)
    k_ref[0] = k.astype(BF16)
    vt = _dot_nt(wvt_ref[...], xb).astype(BF16)
    ones = jnp.ones((V_EXT_ROWS - HEAD_DIM_B, MOBA_BLOCK), BF16)
    for t in range(PROJ_ROWS // MOBA_BLOCK):
        for h in range(N_HEADS_B):
            vext_ref[0, t, h * V_EXT_ROWS:h * V_EXT_ROWS + HEAD_DIM_B, :] = \
                vt[h * HEAD_DIM_B:(h + 1) * HEAD_DIM_B, t * MOBA_BLOCK:(t + 1) * MOBA_BLOCK]
            vext_ref[0, t, h * V_EXT_ROWS + HEAD_DIM_B:(h + 1) * V_EXT_ROWS, :] = ones

    vn = _layer_norm(gv, lng_ref[...], lnb_ref[...]).astype(BF16)
    t_idx = lax.broadcasted_iota(jnp.int32, (CHUNK, 2 * CHUNK), 0)
    s_idx = lax.broadcasted_iota(jnp.int32, (CHUNK, 2 * CHUNK), 1) % CHUNK
    causal = s_idx <= t_idx
    w_pairs = [jnp.where(causal, wsp_ref[p], jnp.zeros((), BF16)) for p in range(N_GROUPS_A // 2)]
    lane = lax.broadcasted_iota(jnp.int32, (CHUNK, LANES), 1)
    left = lane < GROUP_DIM_A
    zero = jnp.zeros((CHUNK, LANES), BF16)
    for c in range(PROJ_ROWS // CHUNK):
        rows = slice(c * CHUNK, (c + 1) * CHUNK)
        mixed = []
        for p in range(N_GROUPS_A // 2):
            blk = vn[rows, p * LANES:(p + 1) * LANES]
            stacked = jnp.concatenate([jnp.where(left, blk, zero), jnp.where(left, zero, blk)], axis=0)
            mixed.append(_dot(w_pairs[p], stacked))
        mixed = jnp.concatenate(mixed, axis=1) + bsp_ref[...]
        ya_ref[rows, :] = (gu[rows, :] * mixed).astype(BF16)


def _projection(x2, w_in_b, cos_t, sin_t, ln_g, ln_b, w_sp, b_sp, bsz, seq):
    n_rows = x2.shape[0]
    tiles_per_seq = seq // PROJ_ROWS
    blocks_per_tile = PROJ_ROWS // MOBA_BLOCK
    const = lambda i: (0, 0)
    resident = pl.Buffered(1)
    seq_tile = lambda i: (i // tiles_per_seq, i % tiles_per_seq, 0)
    return pl.pallas_call(
        _proj_kernel,
        grid=(n_rows // PROJ_ROWS,),
        in_specs=[
            pl.BlockSpec((PROJ_ROWS, D_MODEL), lambda i: (i, 0)),
            pl.BlockSpec((D_MODEL, PROJ_WIDTH), const, pipeline_mode=resident),
            pl.BlockSpec((PROJ_ROWS, LANES), lambda i: (i % tiles_per_seq, 0)),
            pl.BlockSpec((PROJ_ROWS, LANES), lambda i: (i % tiles_per_seq, 0)),
            pl.BlockSpec((1, WIDTH_A), const, pipeline_mode=resident),
            pl.BlockSpec((1, WIDTH_A), const, pipeline_mode=resident),
            pl.BlockSpec((N_GROUPS_A // 2, CHUNK, 2 * CHUNK), lambda i: (0, 0, 0),
                         pipeline_mode=resident),
            pl.BlockSpec((CHUNK, WIDTH_A), const, pipeline_mode=resident),
        ],
        out_specs=[
            pl.BlockSpec((PROJ_ROWS, WIDTH_A), lambda i: (i, 0)),
            pl.BlockSpec((1, PROJ_ROWS, WIDTH_B), seq_tile),
            pl.BlockSpec((1, PROJ_ROWS, WIDTH_B), seq_tile),
            pl.BlockSpec((1, blocks_per_tile, N_HEADS_B * V_EXT_ROWS, MOBA_BLOCK),
                         lambda i: (i // tiles_per_seq, i % tiles_per_seq, 0, 0)),
            pl.BlockSpec((PROJ_ROWS, 2 * D_MODEL), lambda i: (i, 0)),
        ],
        out_shape=[
            jax.ShapeDtypeStruct((n_rows, WIDTH_A), BF16),
            jax.ShapeDtypeStruct((bsz, seq, WIDTH_B), BF16),
            jax.ShapeDtypeStruct((bsz, seq, WIDTH_B), BF16),
            jax.ShapeDtypeStruct((bsz, seq // MOBA_BLOCK, N_HEADS_B * V_EXT_ROWS, MOBA_BLOCK), BF16),
            jax.ShapeDtypeStruct((n_rows, 2 * D_MODEL), BF16),
        ],
        scratch_shapes=[pltpu.VMEM((WIDTH_B, D_MODEL), BF16)],
        compiler_params=pltpu.CompilerParams(
            dimension_semantics=("arbitrary",), vmem_limit_bytes=VMEM_LIMIT),
        name="sgu_moba_projection",
    )(x2, w_in_b, cos_t, sin_t, ln_g, ln_b, w_sp, b_sp)


def _gate_kernel(q_ref, k_ref, pen_ref, *, n_blocks, seq):
    k_mean = jnp.sum(k_ref[0].astype(F32).reshape(n_blocks, MOBA_BLOCK, LANES), axis=1) \
        * np.float32(1.0 / MOBA_BLOCK)
    lane_k = lax.broadcasted_iota(jnp.int32, (n_blocks, LANES), 1)
    blk_row = lax.broadcasted_iota(jnp.int32, (n_blocks, GATE_CHUNK), 0)
    lane_t = lax.broadcasted_iota(jnp.int32, (n_blocks, GATE_CHUNK), 1)
    for h in range(LANES // HEAD_DIM_B):
        km = jnp.where(lane_k // HEAD_DIM_B == h, k_mean, 0.0)
        km_hi = km.astype(BF16)
        km_lo = (km - km_hi.astype(F32)).astype(BF16)
        for c in range(seq // GATE_CHUNK):
            q_c = q_ref[0, c * GATE_CHUNK:(c + 1) * GATE_CHUNK, :]
            gate = _dot_nt(km_hi, q_c) + _dot_nt(km_lo, q_c)
            past = blk_row < (lane_t + c * GATE_CHUNK) // MOBA_BLOCK
            gate = jnp.where(past, gate, NEG_INF)
            sel = jnp.zeros(gate.shape, jnp.bool_)
            for _ in range(MOBA_TOPK):
                best = jnp.max(gate, axis=0, keepdims=True)
                first = jnp.min(jnp.where(gate == best, blk_row, n_blocks), axis=0, keepdims=True)
                pick = blk_row == first
                sel = sel | pick
                gate = jnp.where(pick, NEG_INF, gate)
            pen_ref[0, h, :, c * GATE_CHUNK:(c + 1) * GATE_CHUNK] = \
                jnp.where(sel & past, 0.0, POS_INF).astype(F32)


def _moba_gate(q, k, bsz, seq):
    n_blocks = seq // MOBA_BLOCK
    return pl.pallas_call(
        functools.partial(_gate_kernel, n_blocks=n_blocks, seq=seq),
        grid=(bsz, WIDTH_B // LANES),
        in_specs=[
            pl.BlockSpec((1, seq, LANES), lambda b, p: (b, 0, p)),
            pl.BlockSpec((1, seq, LANES), lambda b, p: (b, 0, p)),
        ],
        out_specs=pl.BlockSpec((1, LANES // HEAD_DIM_B, n_blocks, seq), lambda b, p: (b, p, 0, 0)),
        out_shape=jax.ShapeDtypeStruct((bsz, N_HEADS_B, n_blocks, seq), F32),
        compiler_params=pltpu.CompilerParams(
            dimension_semantics=("arbitrary", "arbitrary"), vmem_limit_bytes=VMEM_LIMIT),
        name="moba_gate",
    )(q, k)


def _attn_kernel(q_ref, k_ref, vext_ref, pen_ref, o_ref, qm_ref, m_ref, acc_ref,
                 sa_ref, samax_ref, sb_ref, sbmax_ref):
    t = pl.program_id(1)
    pair_heads = LANES // HEAD_DIM_B
    n_pairs = N_HEADS_B // pair_heads

    lane_q = lax.broadcasted_iota(jnp.int32, (ATTN_Q, LANES), 1)
    key_pos = lax.broadcasted_iota(jnp.int32, (MOBA_BLOCK, ATTN_Q), 0)
    qry_pos = lax.broadcasted_iota(jnp.int32, (MOBA_BLOCK, ATTN_Q), 1)
    first_half = qry_pos < MOBA_BLOCK

    q_pairs = [q_ref[0, :, p * LANES:(p + 1) * LANES] for p in range(n_pairs)]
    k_first = [k_ref[0, 0:MOBA_BLOCK, p * LANES:(p + 1) * LANES] for p in range(n_pairs)]
    for h in range(N_HEADS_B):
        in_head = lane_q // HEAD_DIM_B == h % pair_heads
        qm = jnp.where(in_head, q_pairs[h // pair_heads], jnp.zeros((), BF16))
        s0 = _dot_nt(k_first[h // pair_heads], qm)
        qm_ref[h] = qm
        sa_ref[h] = s0
        samax_ref[h] = jnp.max(s0, axis=0, keepdims=True)
        m_ref[h] = jnp.full((1, ATTN_Q), M_INIT, F32)
        acc_ref[h] = jnp.zeros((V_EXT_ROWS, ATTN_Q), F32)

    def load_keys(j):
        col0 = pl.multiple_of(j * MOBA_BLOCK, MOBA_BLOCK)
        return [k_ref[0, pl.ds(col0, MOBA_BLOCK), p * LANES:(p + 1) * LANES] for p in range(n_pairs)]

    def load_block(j):
        v_all = [vext_ref[0, j, h * V_EXT_ROWS:(h + 1) * V_EXT_ROWS, :] for h in range(N_HEADS_B)]
        pen_all = [pen_ref[0, h, pl.ds(j, 1), :] for h in range(N_HEADS_B)]
        return v_all, pen_all

    def absorb_head(h, v_h, pen_h, s_ref, smax_ref):
        m_run = m_ref[h]
        m_new = jnp.maximum(m_run, smax_ref[h] - pen_h)
        p_j = jnp.exp2(s_ref[h] - (m_new + pen_h)).astype(BF16)
        m_ref[h] = m_new
        acc_ref[h] = jnp.exp2(m_run - m_new) * acc_ref[h] + _dot(v_h, p_j)

    def fused(k_next, s_next_ref, smax_next_ref, block, s_ref, smax_ref):
        v_all, pen_all = block
        for h in range(N_HEADS_B):
            s_new = _dot_nt(k_next[h // pair_heads], qm_ref[h])
            s_next_ref[h] = s_new
            smax_next_ref[h] = jnp.max(s_new, axis=0, keepdims=True)
            absorb_head(h, v_all[h], pen_all[h], s_ref, smax_ref)

    def kv_pair(jj, carry):
        j = 2 * jj
        keys_b, keys_a = load_keys(j + 1), load_keys(j + 2)
        block_a, block_b = load_block(j), load_block(j + 1)
        fused(keys_b, sb_ref, sbmax_ref, block_a, sa_ref, samax_ref)
        fused(keys_a, sa_ref, samax_ref, block_b, sb_ref, sbmax_ref)
        return carry

    def kv_quad(jq, carry):
        kv_pair(2 * jq, carry)
        kv_pair(2 * jq + 1, carry)
        return carry

    def kv_oct(jo, carry):
        kv_quad(2 * jo, carry)
        kv_quad(2 * jo + 1, carry)
        return carry

    lax.fori_loop(0, t // 4, kv_oct, 0)
    lax.fori_loop(t // 4 * 2, t // 2, kv_quad, 0)
    lax.fori_loop(t // 2 * 2, t, kv_pair, 0)

    keys_last = load_keys(2 * t + 1)
    v_all, pen_all = load_block(2 * t)
    v_last = [vext_ref[0, 2 * t + 1, h * V_EXT_ROWS:(h + 1) * V_EXT_ROWS, :] for h in range(N_HEADS_B)]
    row_first = first_half[0:1, :]
    for h in range(N_HEADS_B):
        sb_ref[h] = _dot_nt(keys_last[h // pair_heads], qm_ref[h])
        s = jnp.where(first_half & (key_pos > qry_pos), NEG_INF, sa_ref[h])
        pen_h = jnp.where(row_first, 0.0, pen_all[h])
        m_run = m_ref[h]
        m_new = jnp.maximum(m_run, jnp.max(s, axis=0, keepdims=True) - pen_h)
        p_j = jnp.exp2(s - (m_new + pen_h)).astype(BF16)
        m_ref[h] = m_new
        acc_ref[h] = jnp.exp2(m_run - m_new) * acc_ref[h] + _dot(v_all[h], p_j)

    for h in range(N_HEADS_B):
        s = jnp.where(key_pos > qry_pos - MOBA_BLOCK, NEG_INF, sb_ref[h])
        m_run = m_ref[h]
        m_new = jnp.maximum(m_run, jnp.max(s, axis=0, keepdims=True))
        p_j = jnp.exp2(s - m_new).astype(BF16)
        acc = jnp.exp2(m_run - m_new) * acc_ref[h] + _dot(v_last[h], p_j)
        out = (acc[0:HEAD_DIM_B, :] / acc[HEAD_DIM_B:HEAD_DIM_B + 1, :]).astype(BF16)
        for half in range(ATTN_Q // MOBA_BLOCK):
            o_ref[0, half, h * HEAD_DIM_B:(h + 1) * HEAD_DIM_B, :] = \
                out[:, half * MOBA_BLOCK:(half + 1) * MOBA_BLOCK]


def _moba_attention(q, k, vext, pen, bsz, seq):
    n_blocks = seq // MOBA_BLOCK
    blocks_per_step = ATTN_Q // MOBA_BLOCK
    resident = pl.Buffered(1)
    return pl.pallas_call(
        _attn_kernel,
        grid=(bsz, seq // ATTN_Q),
        in_specs=[
            pl.BlockSpec((1, ATTN_Q, WIDTH_B), lambda b, t: (b, t, 0)),
            pl.BlockSpec((1, seq, WIDTH_B), lambda b, t: (b, 0, 0), pipeline_mode=resident),
            pl.BlockSpec((1, n_blocks, N_HEADS_B * V_EXT_ROWS, MOBA_BLOCK), lambda b, t: (b, 0, 0, 0),
                         pipeline_mode=resident),
            pl.BlockSpec((1, N_HEADS_B, n_blocks, ATTN_Q), lambda b, t: (b, 0, 0, t)),
        ],
        out_specs=pl.BlockSpec((1, blocks_per_step, WIDTH_B, MOBA_BLOCK), lambda b, t: (b, t, 0, 0)),
        out_shape=jax.ShapeDtypeStruct((bsz, n_blocks, WIDTH_B, MOBA_BLOCK), BF16),
        scratch_shapes=[
            pltpu.VMEM((N_HEADS_B, ATTN_Q, LANES), BF16),
            pltpu.VMEM((N_HEADS_B, 1, ATTN_Q), F32),
            pltpu.VMEM((N_HEADS_B, V_EXT_ROWS, ATTN_Q), F32),
            pltpu.VMEM((N_HEADS_B, MOBA_BLOCK, ATTN_Q), F32),
            pltpu.VMEM((N_HEADS_B, 1, ATTN_Q), F32),
            pltpu.VMEM((N_HEADS_B, MOBA_BLOCK, ATTN_Q), F32),
            pltpu.VMEM((N_HEADS_B, 1, ATTN_Q), F32),
        ],
        compiler_params=pltpu.CompilerParams(
            dimension_semantics=("arbitrary", "arbitrary"), vmem_limit_bytes=VMEM_LIMIT),
        name="moba_attention",
    )(q, k, vext, pen)


def _merge_kernel(x_ref, ya_ref, ybt_ref, g_ref, bg_ref, wa_ref, wb_ref, wo_ref, lng_ref, lnb_ref,
                  o_ref, *, alpha):
    branch_a = _dot(ya_ref[...], wa_ref[...])
    branch_b = jnp.concatenate(
        [_dot_tn(ybt_ref[0, t], wb_ref[...]) for t in range(MERGE_ROWS // MOBA_BLOCK)], axis=0)
    g = g_ref[...].astype(F32) + bg_ref[...]
    merged = (jax.nn.sigmoid(g[:, :D_MODEL]) * branch_a
              + jax.nn.sigmoid(g[:, D_MODEL:]) * branch_b)
    mix = _dot(merged.astype(BF16), wo_ref[...])
    o_ref[...] = _layer_norm(alpha * x_ref[...] + mix, lng_ref[...], lnb_ref[...])


def _merge(x2, ya, ybt, g, b_gate, w_a, w_b, w_o, ln_g, ln_b, alpha, seq):
    n_rows = x2.shape[0]
    tiles_per_seq = seq // MERGE_ROWS
    const = lambda i: (0, 0)
    resident = pl.Buffered(1)
    return pl.pallas_call(
        functools.partial(_merge_kernel, alpha=alpha),
        grid=(n_rows // MERGE_ROWS,),
        in_specs=[
            pl.BlockSpec((MERGE_ROWS, D_MODEL), lambda i: (i, 0)),
            pl.BlockSpec((MERGE_ROWS, WIDTH_A), lambda i: (i, 0)),
            pl.BlockSpec((1, MERGE_ROWS // MOBA_BLOCK, WIDTH_B, MOBA_BLOCK),
                         lambda i: (i // tiles_per_seq, i % tiles_per_seq, 0, 0)),
            pl.BlockSpec((MERGE_ROWS, 2 * D_MODEL), lambda i: (i, 0)),
            pl.BlockSpec((1, 2 * D_MODEL), const, pipeline_mode=resident),
            pl.BlockSpec((WIDTH_A, D_MODEL), const, pipeline_mode=resident),
            pl.BlockSpec((WIDTH_B, D_MODEL), const, pipeline_mode=resident),
            pl.BlockSpec((D_MODEL, D_MODEL), const, pipeline_mode=resident),
            pl.BlockSpec((1, D_MODEL), const, pipeline_mode=resident),
            pl.BlockSpec((1, D_MODEL), const, pipeline_mode=resident),
        ],
        out_specs=pl.BlockSpec((MERGE_ROWS, D_MODEL), lambda i: (i, 0)),
        out_shape=jax.ShapeDtypeStruct((n_rows, D_MODEL), F32),
        compiler_params=pltpu.CompilerParams(
            dimension_semantics=("arbitrary",), vmem_limit_bytes=VMEM_LIMIT),
        name="merge_outproj_ln",
    )(x2, ya, ybt, g, b_gate, w_a, w_b, w_o, ln_g, ln_b)


def _ffn_kernel(x_ref, wup_ref, cw_ref, cb_ref, wdn_ref, lng_ref, lnb_ref, o_ref,
                hbuf_ref, carry_ref, acc_ref, *, alpha, tiles_per_seq):
    i = pl.program_id(0)

    @pl.when(i % tiles_per_seq == 0)
    def _():
        carry_ref[...] = jnp.zeros(carry_ref.shape, F32)

    x = x_ref[...]
    xb = x.astype(BF16)
    acc_ref[...] = jnp.zeros(acc_ref.shape, F32)
    pad = carry_ref.shape[0]
    n_steps = D_FF // FF_TILE

    def cols_of(j, half):
        c0 = half * D_FF + j * FF_TILE
        return slice(c0, c0 + FF_TILE)

    def up_proj(j):
        for half in range(2):
            cols = cols_of(j, half)
            hbuf = hbuf_ref.at[(2 * j + half) % hbuf_ref.shape[0]]
            hbuf[0:pad, :] = carry_ref[:, cols]
            h = _dot(xb, wup_ref[:, cols])
            hbuf[pad:, :] = h
            carry_ref[:, cols] = h[FFN_ROWS - pad:, :]

    def conv_act(j):
        conv = []
        for half in range(2):
            cols = cols_of(j, half)
            hbuf = hbuf_ref.at[(2 * j + half) % hbuf_ref.shape[0]]
            cw = cw_ref[:, cols]
            conv.append(cw[0:1, :] * hbuf[pad - 2:pad - 2 + FFN_ROWS, :]
                        + cw[1:2, :] * hbuf[pad - 1:pad - 1 + FFN_ROWS, :]
                        + cw[2:3, :] * hbuf[pad:, :] + cb_ref[:, cols])
        return (_gelu(conv[0]) * conv[1]).astype(BF16)

    for j in range(min(UP_LOOKAHEAD, n_steps)):
        up_proj(j)
    for j in range(n_steps):
        if j + UP_LOOKAHEAD < n_steps:
            up_proj(j + UP_LOOKAHEAD)
        acc_ref[...] += _dot(conv_act(j), wdn_ref[j * FF_TILE:(j + 1) * FF_TILE, :])
    o_ref[...] = _layer_norm(alpha * x + acc_ref[...], lng_ref[...], lnb_ref[...])


def _conv_ffn(x1, w_up, conv_w, conv_b, w_dn, ln_g, ln_b, alpha, seq):
    n_rows = x1.shape[0]
    tiles_per_seq = seq // FFN_ROWS
    const = lambda i: (0, 0)
    resident = pl.Buffered(1)
    pad = 8
    return pl.pallas_call(
        functools.partial(_ffn_kernel, alpha=alpha, tiles_per_seq=tiles_per_seq),
        grid=(n_rows // FFN_ROWS,),
        in_specs=[
            pl.BlockSpec((FFN_ROWS, D_MODEL), lambda i: (i, 0)),
            pl.BlockSpec((D_MODEL, 2 * D_FF), const, pipeline_mode=resident),
            pl.BlockSpec((CONV_WIDTH, 2 * D_FF), const, pipeline_mode=resident),
            pl.BlockSpec((1, 2 * D_FF), const, pipeline_mode=resident),
            pl.BlockSpec((D_FF, D_MODEL), const, pipeline_mode=resident),
            pl.BlockSpec((1, D_MODEL), const, pipeline_mode=resident),
            pl.BlockSpec((1, D_MODEL), const, pipeline_mode=resident),
        ],
        out_specs=pl.BlockSpec((FFN_ROWS, D_MODEL), lambda i: (i, 0)),
        out_shape=jax.ShapeDtypeStruct((n_rows, D_MODEL), F32),
        scratch_shapes=[
            pltpu.VMEM((2 * (UP_LOOKAHEAD + 1), pad + FFN_ROWS, FF_TILE), F32),
            pltpu.VMEM((pad, 2 * D_FF), F32),
            pltpu.VMEM((FFN_ROWS, D_MODEL), F32),
        ],
        compiler_params=pltpu.CompilerParams(
            dimension_semantics=("arbitrary",), vmem_limit_bytes=VMEM_LIMIT),
        name="conv_ffn_ln",
    )(x1, w_up, conv_w, conv_b, w_dn, ln_g, ln_b)


def _rope_tables(seq):
    half = HEAD_DIM_B // 2
    inv_freq = np.float32(ROPE_THETA) ** (-np.arange(half, dtype=np.float32) / np.float32(half))
    ang = np.arange(seq, dtype=np.float32)[:, None] * inv_freq[None, :]
    cos = np.tile(np.cos(ang), (1, LANES // half)).astype(np.float32)
    sin = np.tile(np.sin(ang), (1, LANES // half)).astype(np.float32)
    first_half = (np.arange(LANES) % HEAD_DIM_B) < half
    return jnp.asarray(cos), jnp.asarray(np.where(first_half[None, :], -sin, sin))


def kernel(x, w_in, b_gate, sgu_ln_g, sgu_ln_b, w_spatial, b_spatial, w_branch_a, w_branch_b,
           w_out, ln1_g, ln1_b, w_up, conv_w, conv_b, w_down, ln2_g, ln2_b):
    bsz, seq, d_model = x.shape
    depth = w_in.shape[0]
    assert d_model == D_MODEL and all(seq % t == 0 for t in (PROJ_ROWS, MERGE_ROWS, FFN_ROWS))
    assert all(t % MOBA_BLOCK == 0 for t in (PROJ_ROWS, MERGE_ROWS))
    alpha = float((2.0 * depth) ** 0.25)
    cos_t, sin_t = _rope_tables(seq)
    x2 = x.reshape(bsz * seq, d_model)
    for l in range(depth):
        w_in_b = w_in[l].astype(BF16)
        w_sp = w_spatial[l].astype(BF16).reshape(N_GROUPS_A // 2, 2, CHUNK, CHUNK)
        w_sp = w_sp.transpose(0, 2, 1, 3).reshape(N_GROUPS_A // 2, CHUNK, 2 * CHUNK)
        b_sp = jnp.repeat(b_spatial[l].T, GROUP_DIM_A, axis=1)
        ya, q, k, vext, g = _projection(
            x2, w_in_b, cos_t, sin_t, sgu_ln_g[l][None, :], sgu_ln_b[l][None, :],
            w_sp, b_sp, bsz, seq)
        pen = _moba_gate(q, k, bsz, seq)
        ybt = _moba_attention(q, k, vext, pen, bsz, seq)
        x1 = _merge(x2, ya, ybt, g, b_gate[l][None, :], w_branch_a[l].astype(BF16),
                    w_branch_b[l].astype(BF16), w_out[l].astype(BF16),
                    ln1_g[l][None, :], ln1_b[l][None, :], alpha, seq)
        x2 = _conv_ffn(x1, w_up[l].astype(BF16), conv_w[l], conv_b[l][None, :],
                       w_down[l].astype(BF16), ln2_g[l][None, :], ln2_b[l][None, :], alpha, seq)
    return x2.reshape(bsz, seq, d_model)
```

```python
import functools

import jax
import jax.numpy as jnp
import numpy as np
from jax import lax
from jax.experimental import pallas as pl
from jax.experimental.pallas import tpu as pltpu

F32 = jnp.float32
BF16 = jnp.bfloat16

D_MODEL = 1024
N_GROUPS_A = 8
GROUP_DIM_A = 64
WIDTH_A = N_GROUPS_A * GROUP_DIM_A
CHUNK = 128
N_HEADS_B = 8
HEAD_DIM_B = 64
WIDTH_B = N_HEADS_B * HEAD_DIM_B
MOBA_BLOCK = 256
MOBA_TOPK = 3
ROPE_THETA = 10000.0
D_FF = 2816
CONV_WIDTH = 3
LN_EPS = 1e-5
PROJ_WIDTH = 2 * WIDTH_A + 3 * WIDTH_B + 2 * D_MODEL

LANES = 128
PROJ_ROWS = 1024
MERGE_ROWS = 1024
FF_TILE = 256
FFN_ROWS = 512
UP_LOOKAHEAD = 5
GATE_CHUNK = 1024
V_EXT_ROWS = HEAD_DIM_B + 16
VMEM_LIMIT = 56 * 1024 * 1024
NEG_INF = float("-inf")
POS_INF = float("inf")
Q_SCALE = np.float32(HEAD_DIM_B ** -0.5 * np.log2(np.e))


def _gelu(x):
    return 0.5 * x * (1.0 + lax.erf(x * np.float32(np.sqrt(0.5))))


def _layer_norm(x, g, b):
    mu = jnp.mean(x, axis=-1, keepdims=True)
    xc = x - mu
    var = jnp.mean(xc * xc, axis=-1, keepdims=True)
    return xc * lax.rsqrt(var + LN_EPS) * g + b


def _dot(a, b):
    return jnp.dot(a, b, preferred_element_type=F32)


def _dot_nt(a, b):
    return lax.dot_general(a, b, (((1,), (1,)), ((), ())), preferred_element_type=F32)


def _dot_tn(a, b):
    return lax.dot_general(a, b, (((0,), (0,)), ((), ())), preferred_element_type=F32)


def _rope(t, cos, sin_signed):
    lane = lax.broadcasted_iota(jnp.int32, (t.shape[0], LANES), 1)
    first_half = (lane % HEAD_DIM_B) < (HEAD_DIM_B // 2)
    out = []
    for c in range(t.shape[1] // LANES):
        blk = t[:, c * LANES:(c + 1) * LANES]
        up = pltpu.roll(blk, LANES - HEAD_DIM_B // 2, 1)
        down = pltpu.roll(blk, HEAD_DIM_B // 2, 1)
        partner = jnp.where(first_half, up, down)
        out.append(blk * cos + partner * sin_signed)
    return jnp.concatenate(out, axis=1)


def _proj_kernel(x_ref, w_ref, cos_ref, sin_ref, lng_ref, lnb_ref, wsp_ref, bsp_ref,
                 ya_ref, q_ref, k_ref, vext_ref, g_ref, wvt_ref):
    @pl.when(pl.program_id(0) == 0)
    def _():
        w_v = w_ref[:, 2 * WIDTH_A + 2 * WIDTH_B:2 * WIDTH_A + 3 * WIDTH_B].astype(F32)
        wvt_ref[...] = w_v.T.astype(BF16)

    xb = x_ref[...].astype(BF16)
    gu = _gelu(_dot(xb, w_ref[:, 0:WIDTH_A]))
    gv = _gelu(_dot(xb, w_ref[:, WIDTH_A:2 * WIDTH_A]))
    g_ref[...] = _dot(xb, w_ref[:, 2 * WIDTH_A + 3 * WIDTH_B:]).astype(BF16)
    cos = cos_ref[...]
    sin = sin_ref[...]
    q = _rope(_dot(xb, w_ref[:, 2 * WIDTH_A:2 * WIDTH_A + WIDTH_B]), cos, sin)
    q_ref[0] = (q * Q_SCALE).astype(BF16)
    k = _rope(_dot(xb, w_ref[:, 2 * WIDTH_A + WIDTH_B:2 * WIDTH_A + 2 * WIDTH_B]), cos, sin)
    k_ref[0] = k.astype(BF16)
    vt = _dot_nt(wvt_ref[...], xb).astype(BF16)
    ones = jnp.ones((V_EXT_ROWS - HEAD_DIM_B, MOBA_BLOCK), BF16)
    for t in range(PROJ_ROWS // MOBA_BLOCK):
        for h in range(N_HEADS_B):
            vext_ref[0, t, h * V_EXT_ROWS:h * V_EXT_ROWS + HEAD_DIM_B, :] = \
                vt[h * HEAD_DIM_B:(h + 1) * HEAD_DIM_B, t * MOBA_BLOCK:(t + 1) * MOBA_BLOCK]
            vext_ref[0, t, h * V_EXT_ROWS + HEAD_DIM_B:(h + 1) * V_EXT_ROWS, :] = ones

    vn = _layer_norm(gv, lng_ref[...], lnb_ref[...]).astype(BF16)
    t_idx = lax.broadcasted_iota(jnp.int32, (CHUNK, 2 * CHUNK), 0)
    s_idx = lax.broadcasted_iota(jnp.int32, (CHUNK, 2 * CHUNK), 1) % CHUNK
    causal = s_idx <= t_idx
    w_pairs = [jnp.where(causal, wsp_ref[p], jnp.zeros((), BF16)) for p in range(N_GROUPS_A // 2)]
    lane = lax.broadcasted_iota(jnp.int32, (CHUNK, LANES), 1)
    left = lane < GROUP_DIM_A
    zero = jnp.zeros((CHUNK, LANES), BF16)
    for c in range(PROJ_ROWS // CHUNK):
        rows = slice(c * CHUNK, (c + 1) * CHUNK)
        mixed = []
        for p in range(N_GROUPS_A // 2):
            blk = vn[rows, p * LANES:(p + 1) * LANES]
            stacked = jnp.concatenate([jnp.where(left, blk, zero), jnp.where(left, zero, blk)], axis=0)
            mixed.append(_dot(w_pairs[p], stacked))
        mixed = jnp.concatenate(mixed, axis=1) + bsp_ref[...]
        ya_ref[rows, :] = (gu[rows, :] * mixed).astype(BF16)


def _projection(x2, w_in_b, cos_t, sin_t, ln_g, ln_b, w_sp, b_sp, bsz, seq):
    n_rows = x2.shape[0]
    tiles_per_seq = seq // PROJ_ROWS
    blocks_per_tile = PROJ_ROWS // MOBA_BLOCK
    const = lambda i: (0, 0)
    resident = pl.Buffered(1)
    seq_tile = lambda i: (i // tiles_per_seq, i % tiles_per_seq, 0)
    return pl.pallas_call(
        _proj_kernel,
        grid=(n_rows // PROJ_ROWS,),
        in_specs=[
            pl.BlockSpec((PROJ_ROWS, D_MODEL), lambda i: (i, 0)),
            pl.BlockSpec((D_MODEL, PROJ_WIDTH), const, pipeline_mode=resident),
            pl.BlockSpec((PROJ_ROWS, LANES), lambda i: (i % tiles_per_seq, 0)),
            pl.BlockSpec((PROJ_ROWS, LANES), lambda i: (i % tiles_per_seq, 0)),
            pl.BlockSpec((1, WIDTH_A), const, pipeline_mode=resident),
            pl.BlockSpec((1, WIDTH_A), const, pipeline_mode=resident),
            pl.BlockSpec((N_GROUPS_A // 2, CHUNK, 2 * CHUNK), lambda i: (0, 0, 0),
                         pipeline_mode=resident),
            pl.BlockSpec((CHUNK, WIDTH_A), const, pipeline_mode=resident),
        ],
        out_specs=[
            pl.BlockSpec((PROJ_ROWS, WIDTH_A), lambda i: (i, 0)),
            pl.BlockSpec((1, PROJ_ROWS, WIDTH_B), seq_tile),
            pl.BlockSpec((1, PROJ_ROWS, WIDTH_B), seq_tile),
            pl.BlockSpec((1, blocks_per_tile, N_HEADS_B * V_EXT_ROWS, MOBA_BLOCK),
                         lambda i: (i // tiles_per_seq, i % tiles_per_seq, 0, 0)),
            pl.BlockSpec((PROJ_ROWS, 2 * D_MODEL), lambda i: (i, 0)),
        ],
        out_shape=[
            jax.ShapeDtypeStruct((n_rows, WIDTH_A), BF16),
            jax.ShapeDtypeStruct((bsz, seq, WIDTH_B), BF16),
            jax.ShapeDtypeStruct((bsz, seq, WIDTH_B), BF16),
            jax.ShapeDtypeStruct((bsz, seq // MOBA_BLOCK, N_HEADS_B * V_EXT_ROWS, MOBA_BLOCK), BF16),
            jax.ShapeDtypeStruct((n_rows, 2 * D_MODEL), BF16),
        ],
        scratch_shapes=[pltpu.VMEM((WIDTH_B, D_MODEL), BF16)],
        compiler_params=pltpu.CompilerParams(
            dimension_semantics=("arbitrary",), vmem_limit_bytes=VMEM_LIMIT),
        name="sgu_moba_projection",
    )(x2, w_in_b, cos_t, sin_t, ln_g, ln_b, w_sp, b_sp)


def _gate_kernel(q_ref, k_ref, pen_ref, *, n_blocks, seq):
    k_mean = jnp.sum(k_ref[0].astype(F32).reshape(n_blocks, MOBA_BLOCK, LANES), axis=1) \
        * np.float32(1.0 / MOBA_BLOCK)
    lane_k = lax.broadcasted_iota(jnp.int32, (n_blocks, LANES), 1)
    blk_row = lax.broadcasted_iota(jnp.int32, (n_blocks, GATE_CHUNK), 0)
    lane_t = lax.broadcasted_iota(jnp.int32, (n_blocks, GATE_CHUNK), 1)
    for h in range(LANES // HEAD_DIM_B):
        km = jnp.where(lane_k // HEAD_DIM_B == h, k_mean, 0.0)
        km_hi = km.astype(BF16)
        km_lo = (km - km_hi.astype(F32)).astype(BF16)
        for c in range(seq // GATE_CHUNK):
            q_c = q_ref[0, c * GATE_CHUNK:(c + 1) * GATE_CHUNK, :]
            gate = _dot_nt(km_hi, q_c) + _dot_nt(km_lo, q_c)
            past = blk_row < (lane_t + c * GATE_CHUNK) // MOBA_BLOCK
            gate = jnp.where(past, gate, NEG_INF)
            sel = jnp.zeros(gate.shape, jnp.bool_)
            for _ in range(MOBA_TOPK):
                best = jnp.max(gate, axis=0, keepdims=True)
                first = jnp.min(jnp.where(gate == best, blk_row, n_blocks), axis=0, keepdims=True)
                pick = blk_row == first
                sel = sel | pick
                gate = jnp.where(pick, NEG_INF, gate)
            pen_ref[0, h, :, c * GATE_CHUNK:(c + 1) * GATE_CHUNK] = \
                jnp.where(sel & past, 0.0, POS_INF).astype(F32)


def _moba_gate(q, k, bsz, seq):
    n_blocks = seq // MOBA_BLOCK
    return pl.pallas_call(
        functools.partial(_gate_kernel, n_blocks=n_blocks, seq=seq),
        grid=(bsz, WIDTH_B // LANES),
        in_specs=[
            pl.BlockSpec((1, seq, LANES), lambda b, p: (b, 0, p)),
            pl.BlockSpec((1, seq, LANES), lambda b, p: (b, 0, p)),
        ],
        out_specs=pl.BlockSpec((1, LANES // HEAD_DIM_B, n_blocks, seq), lambda b, p: (b, p, 0, 0)),
        out_shape=jax.ShapeDtypeStruct((bsz, N_HEADS_B, n_blocks, seq), F32),
        compiler_params=pltpu.CompilerParams(
            dimension_semantics=("arbitrary", "arbitrary"), vmem_limit_bytes=VMEM_LIMIT),
        name="moba_gate",
    )(q, k)


def _attn_kernel(q_ref, k_ref, vext_ref, pen_ref, o_ref, qm_ref, m_ref, acc_ref,
                 sa_ref, samax_ref, sb_ref, sbmax_ref):
    i = pl.program_id(1)
    pair_heads = LANES // HEAD_DIM_B
    n_pairs = N_HEADS_B // pair_heads

    row0 = pl.multiple_of(i * MOBA_BLOCK, MOBA_BLOCK)
    lane_q = lax.broadcasted_iota(jnp.int32, (MOBA_BLOCK, LANES), 1)
    key_pos = lax.broadcasted_iota(jnp.int32, (MOBA_BLOCK, MOBA_BLOCK), 0)
    qry_pos = lax.broadcasted_iota(jnp.int32, (MOBA_BLOCK, MOBA_BLOCK), 1)

    q_pairs = [q_ref[0, :, p * LANES:(p + 1) * LANES] for p in range(n_pairs)]
    k_own = [k_ref[0, pl.ds(row0, MOBA_BLOCK), p * LANES:(p + 1) * LANES] for p in range(n_pairs)]
    v_own = [vext_ref[0, i, h * V_EXT_ROWS:(h + 1) * V_EXT_ROWS, :] for h in range(N_HEADS_B)]
    qm_all, s_all = [], []
    for h in range(N_HEADS_B):
        in_head = lane_q // HEAD_DIM_B == h % pair_heads
        qm = jnp.where(in_head, q_pairs[h // pair_heads], jnp.zeros((), BF16))
        qm_all.append(qm)
        s_all.append(_dot_nt(k_own[h // pair_heads], qm))
    k_first = [k_ref[0, 0:MOBA_BLOCK, p * LANES:(p + 1) * LANES] for p in range(n_pairs)]
    m_all, acc_all, s0_all = [], [], []
    for h in range(N_HEADS_B):
        s0_all.append(_dot_nt(k_first[h // pair_heads], qm_all[h]))
        s = jnp.where(key_pos <= qry_pos, s_all[h], NEG_INF)
        m0 = jnp.max(s, axis=0, keepdims=True)
        m_all.append(m0)
        acc_all.append(_dot(v_own[h], jnp.exp2(s - m0).astype(BF16)))
    for h in range(N_HEADS_B):
        qm_ref[h] = qm_all[h]
        m_ref[h] = m_all[h]
        acc_ref[h] = acc_all[h]
        sa_ref[h] = s0_all[h]
        samax_ref[h] = jnp.max(s0_all[h], axis=0, keepdims=True)

    def load_keys(j):
        col0 = pl.multiple_of(j * MOBA_BLOCK, MOBA_BLOCK)
        return [k_ref[0, pl.ds(col0, MOBA_BLOCK), p * LANES:(p + 1) * LANES] for p in range(n_pairs)]

    def load_block(j):
        v_all = [vext_ref[0, j, h * V_EXT_ROWS:(h + 1) * V_EXT_ROWS, :] for h in range(N_HEADS_B)]
        pen_all = [pen_ref[0, h, pl.ds(j, 1), :] for h in range(N_HEADS_B)]
        return v_all, pen_all

    def absorb_head(h, v_h, pen_h, s_ref, smax_ref):
        m_run = m_ref[h]
        m_new = jnp.maximum(m_run, smax_ref[h] - pen_h)
        p_j = jnp.exp2(s_ref[h] - (m_new + pen_h)).astype(BF16)
        m_ref[h] = m_new
        acc_ref[h] = jnp.exp2(m_run - m_new) * acc_ref[h] + _dot(v_h, p_j)

    def fused(k_next, s_next_ref, smax_next_ref, block, s_ref, smax_ref):
        v_all, pen_all = block
        for h in range(N_HEADS_B):
            s_new = _dot_nt(k_next[h // pair_heads], qm_ref[h])
            s_next_ref[h] = s_new
            smax_next_ref[h] = jnp.max(s_new, axis=0, keepdims=True)
            absorb_head(h, v_all[h], pen_all[h], s_ref, smax_ref)

    def kv_pair(jj, carry):
        j = 2 * jj
        keys_b, keys_a = load_keys(j + 1), load_keys(j + 2)
        block_a, block_b = load_block(j), load_block(j + 1)
        fused(keys_b, sb_ref, sbmax_ref, block_a, sa_ref, samax_ref)
        fused(keys_a, sa_ref, samax_ref, block_b, sb_ref, sbmax_ref)
        return carry

    def kv_quad(jq, carry):
        kv_pair(2 * jq, carry)
        kv_pair(2 * jq + 1, carry)
        return carry

    def kv_oct(jo, carry):
        kv_quad(2 * jo, carry)
        kv_quad(2 * jo + 1, carry)
        return carry

    lax.fori_loop(0, i // 8, kv_oct, 0)
    lax.fori_loop(i // 8 * 2, i // 4, kv_quad, 0)
    lax.fori_loop(i // 4 * 2, i // 2, kv_pair, 0)

    @pl.when(i % 2 == 1)
    def _():
        v_all, pen_all = load_block(i - 1)
        for h in range(N_HEADS_B):
            absorb_head(h, v_all[h], pen_all[h], sa_ref, samax_ref)

    for h in range(N_HEADS_B):
        acc = acc_ref[h]
        out = acc[0:HEAD_DIM_B, :] / acc[HEAD_DIM_B:HEAD_DIM_B + 1, :]
        o_ref[0, 0, h * HEAD_DIM_B:(h + 1) * HEAD_DIM_B, :] = out.astype(BF16)


def _moba_attention(q, k, vext, pen, bsz, seq):
    n_blocks = seq // MOBA_BLOCK
    resident = pl.Buffered(1)
    return pl.pallas_call(
        _attn_kernel,
        grid=(bsz, n_blocks),
        in_specs=[
            pl.BlockSpec((1, MOBA_BLOCK, WIDTH_B), lambda b, i: (b, i, 0)),
            pl.BlockSpec((1, seq, WIDTH_B), lambda b, i: (b, 0, 0), pipeline_mode=resident),
            pl.BlockSpec((1, n_blocks, N_HEADS_B * V_EXT_ROWS, MOBA_BLOCK), lambda b, i: (b, 0, 0, 0),
                         pipeline_mode=resident),
            pl.BlockSpec((1, N_HEADS_B, n_blocks, MOBA_BLOCK), lambda b, i: (b, 0, 0, i)),
        ],
        out_specs=pl.BlockSpec((1, 1, WIDTH_B, MOBA_BLOCK), lambda b, i: (b, i, 0, 0)),
        out_shape=jax.ShapeDtypeStruct((bsz, n_blocks, WIDTH_B, MOBA_BLOCK), BF16),
        scratch_shapes=[
            pltpu.VMEM((N_HEADS_B, MOBA_BLOCK, LANES), BF16),
            pltpu.VMEM((N_HEADS_B, 1, MOBA_BLOCK), F32),
            pltpu.VMEM((N_HEADS_B, V_EXT_ROWS, MOBA_BLOCK), F32),
            pltpu.VMEM((N_HEADS_B, MOBA_BLOCK, MOBA_BLOCK), F32),
            pltpu.VMEM((N_HEADS_B, 1, MOBA_BLOCK), F32),
            pltpu.VMEM((N_HEADS_B, MOBA_BLOCK, MOBA_BLOCK), F32),
            pltpu.VMEM((N_HEADS_B, 1, MOBA_BLOCK), F32),
        ],
        compiler_params=pltpu.CompilerParams(
            dimension_semantics=("arbitrary", "arbitrary"), vmem_limit_bytes=VMEM_LIMIT),
        name="moba_attention",
    )(q, k, vext, pen)


def _merge_kernel(x_ref, ya_ref, ybt_ref, g_ref, bg_ref, wa_ref, wb_ref, wo_ref, lng_ref, lnb_ref,
                  o_ref, *, alpha):
    branch_a = _dot(ya_ref[...], wa_ref[...])
    branch_b = jnp.concatenate(
        [_dot_tn(ybt_ref[0, t], wb_ref[...]) for t in range(MERGE_ROWS // MOBA_BLOCK)], axis=0)
    g = g_ref[...].astype(F32) + bg_ref[...]
    merged = (jax.nn.sigmoid(g[:, :D_MODEL]) * branch_a
              + jax.nn.sigmoid(g[:, D_MODEL:]) * branch_b)
    mix = _dot(merged.astype(BF16), wo_ref[...])
    o_ref[...] = _layer_norm(alpha * x_ref[...] + mix, lng_ref[...], lnb_ref[...])


def _merge(x2, ya, ybt, g, b_gate, w_a, w_b, w_o, ln_g, ln_b, alpha, seq):
    n_rows = x2.shape[0]
    tiles_per_seq = seq // MERGE_ROWS
    const = lambda i: (0, 0)
    resident = pl.Buffered(1)
    return pl.pallas_call(
        functools.partial(_merge_kernel, alpha=alpha),
        grid=(n_rows // MERGE_ROWS,),
        in_specs=[
            pl.BlockSpec((MERGE_ROWS, D_MODEL), lambda i: (i, 0)),
            pl.BlockSpec((MERGE_ROWS, WIDTH_A), lambda i: (i, 0)),
            pl.BlockSpec((1, MERGE_ROWS // MOBA_BLOCK, WIDTH_B, MOBA_BLOCK),
                         lambda i: (i // tiles_per_seq, i % tiles_per_seq, 0, 0)),
            pl.BlockSpec((MERGE_ROWS, 2 * D_MODEL), lambda i: (i, 0)),
            pl.BlockSpec((1, 2 * D_MODEL), const, pipeline_mode=resident),
            pl.BlockSpec((WIDTH_A, D_MODEL), const, pipeline_mode=resident),
            pl.BlockSpec((WIDTH_B, D_MODEL), const, pipeline_mode=resident),
            pl.BlockSpec((D_MODEL, D_MODEL), const, pipeline_mode=resident),
            pl.BlockSpec((1, D_MODEL), const, pipeline_mode=resident),
            pl.BlockSpec((1, D_MODEL), const, pipeline_mode=resident),
        ],
        out_specs=pl.BlockSpec((MERGE_ROWS, D_MODEL), lambda i: (i, 0)),
        out_shape=jax.ShapeDtypeStruct((n_rows, D_MODEL), F32),
        compiler_params=pltpu.CompilerParams(
            dimension_semantics=("arbitrary",), vmem_limit_bytes=VMEM_LIMIT),
        name="merge_outproj_ln",
    )(x2, ya, ybt, g, b_gate, w_a, w_b, w_o, ln_g, ln_b)


def _ffn_kernel(x_ref, wup_ref, cw_ref, cb_ref, wdn_ref, lng_ref, lnb_ref, o_ref,
                hbuf_ref, carry_ref, acc_ref, *, alpha, tiles_per_seq):
    i = pl.program_id(0)

    @pl.when(i % tiles_per_seq == 0)
    def _():
        carry_ref[...] = jnp.zeros(carry_ref.shape, F32)

    x = x_ref[...]
    xb = x.astype(BF16)
    acc_ref[...] = jnp.zeros(acc_ref.shape, F32)
    pad = carry_ref.shape[0]
    n_steps = D_FF // FF_TILE

    def cols_of(j, half):
        c0 = half * D_FF + j * FF_TILE
        return slice(c0, c0 + FF_TILE)

    def up_proj(j):
        for half in range(2):
            cols = cols_of(j, half)
            hbuf = hbuf_ref.at[(2 * j + half) % hbuf_ref.shape[0]]
            hbuf[0:pad, :] = carry_ref[:, cols]
            h = _dot(xb, wup_ref[:, cols])
            hbuf[pad:, :] = h
            carry_ref[:, cols] = h[FFN_ROWS - pad:, :]

    def conv_act(j):
        conv = []
        for half in range(2):
            cols = cols_of(j, half)
            hbuf = hbuf_ref.at[(2 * j + half) % hbuf_ref.shape[0]]
            cw = cw_ref[:, cols]
            conv.append(cw[0:1, :] * hbuf[pad - 2:pad - 2 + FFN_ROWS, :]
                        + cw[1:2, :] * hbuf[pad - 1:pad - 1 + FFN_ROWS, :]
                        + cw[2:3, :] * hbuf[pad:, :] + cb_ref[:, cols])
        return (_gelu(conv[0]) * conv[1]).astype(BF16)

    for j in range(min(UP_LOOKAHEAD, n_steps)):
        up_proj(j)
    for j in range(n_steps):
        if j + UP_LOOKAHEAD < n_steps:
            up_proj(j + UP_LOOKAHEAD)
        acc_ref[...] += _dot(conv_act(j), wdn_ref[j * FF_TILE:(j + 1) * FF_TILE, :])
    o_ref[...] = _layer_norm(alpha * x + acc_ref[...], lng_ref[...], lnb_ref[...])


def _conv_ffn(x1, w_up, conv_w, conv_b, w_dn, ln_g, ln_b, alpha, seq):
    n_rows = x1.shape[0]
    tiles_per_seq = seq // FFN_ROWS
    const = lambda i: (0, 0)
    resident = pl.Buffered(1)
    pad = 8
    return pl.pallas_call(
        functools.partial(_ffn_kernel, alpha=alpha, tiles_per_seq=tiles_per_seq),
        grid=(n_rows // FFN_ROWS,),
        in_specs=[
            pl.BlockSpec((FFN_ROWS, D_MODEL), lambda i: (i, 0)),
            pl.BlockSpec((D_MODEL, 2 * D_FF), const, pipeline_mode=resident),
            pl.BlockSpec((CONV_WIDTH, 2 * D_FF), const, pipeline_mode=resident),
            pl.BlockSpec((1, 2 * D_FF), const, pipeline_mode=resident),
            pl.BlockSpec((D_FF, D_MODEL), const, pipeline_mode=resident),
            pl.BlockSpec((1, D_MODEL), const, pipeline_mode=resident),
            pl.BlockSpec((1, D_MODEL), const, pipeline_mode=resident),
        ],
        out_specs=pl.BlockSpec((FFN_ROWS, D_MODEL), lambda i: (i, 0)),
        out_shape=jax.ShapeDtypeStruct((n_rows, D_MODEL), F32),
        scratch_shapes=[
            pltpu.VMEM((2 * (UP_LOOKAHEAD + 1), pad + FFN_ROWS, FF_TILE), F32),
            pltpu.VMEM((pad, 2 * D_FF), F32),
            pltpu.VMEM((FFN_ROWS, D_MODEL), F32),
        ],
        compiler_params=pltpu.CompilerParams(
            dimension_semantics=("arbitrary",), vmem_limit_bytes=VMEM_LIMIT),
        name="conv_ffn_ln",
    )(x1, w_up, conv_w, conv_b, w_dn, ln_g, ln_b)


def _rope_tables(seq):
    half = HEAD_DIM_B // 2
    inv_freq = np.float32(ROPE_THETA) ** (-np.arange(half, dtype=np.float32) / np.float32(half))
    ang = np.arange(seq, dtype=np.float32)[:, None] * inv_freq[None, :]
    cos = np.tile(np.cos(ang), (1, LANES // half)).astype(np.float32)
    sin = np.tile(np.sin(ang), (1, LANES // half)).astype(np.float32)
    first_half = (np.arange(LANES) % HEAD_DIM_B) < half
    return jnp.asarray(cos), jnp.asarray(np.where(first_half[None, :], -sin, sin))


def kernel(x, w_in, b_gate, sgu_ln_g, sgu_ln_b, w_spatial, b_spatial, w_branch_a, w_branch_b,
           w_out, ln1_g, ln1_b, w_up, conv_w, conv_b, w_down, ln2_g, ln2_b):
    bsz, seq, d_model = x.shape
    depth = w_in.shape[0]
    assert d_model == D_MODEL and all(seq % t == 0 for t in (PROJ_ROWS, MERGE_ROWS, FFN_ROWS))
    assert all(t % MOBA_BLOCK == 0 for t in (PROJ_ROWS, MERGE_ROWS))
    alpha = float((2.0 * depth) ** 0.25)
    cos_t, sin_t = _rope_tables(seq)
    x2 = x.reshape(bsz * seq, d_model)
    for l in range(depth):
        w_in_b = w_in[l].astype(BF16)
        w_sp = w_spatial[l].astype(BF16).reshape(N_GROUPS_A // 2, 2, CHUNK, CHUNK)
        w_sp = w_sp.transpose(0, 2, 1, 3).reshape(N_GROUPS_A // 2, CHUNK, 2 * CHUNK)
        b_sp = jnp.repeat(b_spatial[l].T, GROUP_DIM_A, axis=1)
        ya, q, k, vext, g = _projection(
            x2, w_in_b, cos_t, sin_t, sgu_ln_g[l][None, :], sgu_ln_b[l][None, :],
            w_sp, b_sp, bsz, seq)
        pen = _moba_gate(q, k, bsz, seq)
        ybt = _moba_attention(q, k, vext, pen, bsz, seq)
        x1 = _merge(x2, ya, ybt, g, b_gate[l][None, :], w_branch_a[l].astype(BF16),
                    w_branch_b[l].astype(BF16), w_out[l].astype(BF16),
                    ln1_g[l][None, :], ln1_b[l][None, :], alpha, seq)
        x2 = _conv_ffn(x1, w_up[l].astype(BF16), conv_w[l], conv_b[l][None, :],
                       w_down[l].astype(BF16), ln2_g[l][None, :], ln2_b[l][None, :], alpha, seq)
    return x2.reshape(bsz, seq, d_model)
```

```python
import functools

import jax
import jax.numpy as jnp
import numpy as np
from jax import lax
from jax.experimental import pallas as pl
from jax.experimental.pallas import tpu as pltpu

F32 = jnp.float32
BF16 = jnp.bfloat16

D_MODEL = 1024
N_GROUPS_A = 8
GROUP_DIM_A = 64
WIDTH_A = N_GROUPS_A * GROUP_DIM_A
CHUNK = 128
N_HEADS_B = 8
HEAD_DIM_B = 64
WIDTH_B = N_HEADS_B * HEAD_DIM_B
MOBA_BLOCK = 256
MOBA_TOPK = 3
ROPE_THETA = 10000.0
D_FF = 2816
CONV_WIDTH = 3
LN_EPS = 1e-5
PROJ_WIDTH = 2 * WIDTH_A + 3 * WIDTH_B + 2 * D_MODEL

LANES = 128
PROJ_ROWS = 1024
MERGE_ROWS = 1024
FF_TILE = 256
FFN_ROWS = 1024
UP_LOOKAHEAD = 3
GATE_CHUNK = 1024
V_EXT_ROWS = HEAD_DIM_B + 16
VMEM_LIMIT = 56 * 1024 * 1024
NEG_INF = float("-inf")
POS_INF = float("inf")
Q_SCALE = np.float32(HEAD_DIM_B ** -0.5 * np.log2(np.e))


def _gelu(x):
    return 0.5 * x * (1.0 + lax.erf(x * np.float32(np.sqrt(0.5))))


def _layer_norm(x, g, b):
    mu = jnp.mean(x, axis=-1, keepdims=True)
    xc = x - mu
    var = jnp.mean(xc * xc, axis=-1, keepdims=True)
    return xc * lax.rsqrt(var + LN_EPS) * g + b


def _dot(a, b):
    return jnp.dot(a, b, preferred_element_type=F32)


def _dot_nt(a, b):
    return lax.dot_general(a, b, (((1,), (1,)), ((), ())), preferred_element_type=F32)


def _dot_tn(a, b):
    return lax.dot_general(a, b, (((0,), (0,)), ((), ())), preferred_element_type=F32)


def _rope(t, cos, sin_signed):
    lane = lax.broadcasted_iota(jnp.int32, (t.shape[0], LANES), 1)
    first_half = (lane % HEAD_DIM_B) < (HEAD_DIM_B // 2)
    out = []
    for c in range(t.shape[1] // LANES):
        blk = t[:, c * LANES:(c + 1) * LANES]
        up = pltpu.roll(blk, LANES - HEAD_DIM_B // 2, 1)
        down = pltpu.roll(blk, HEAD_DIM_B // 2, 1)
        partner = jnp.where(first_half, up, down)
        out.append(blk * cos + partner * sin_signed)
    return jnp.concatenate(out, axis=1)


def _proj_kernel(x_ref, w_ref, cos_ref, sin_ref, lng_ref, lnb_ref, wsp_ref, bsp_ref,
                 ya_ref, q_ref, k_ref, vext_ref, g_ref, wvt_ref):
    @pl.when(pl.program_id(0) == 0)
    def _():
        w_v = w_ref[:, 2 * WIDTH_A + 2 * WIDTH_B:2 * WIDTH_A + 3 * WIDTH_B].astype(F32)
        wvt_ref[...] = w_v.T.astype(BF16)

    xb = x_ref[...].astype(BF16)
    gu = _gelu(_dot(xb, w_ref[:, 0:WIDTH_A]))
    gv = _gelu(_dot(xb, w_ref[:, WIDTH_A:2 * WIDTH_A]))
    g_ref[...] = _dot(xb, w_ref[:, 2 * WIDTH_A + 3 * WIDTH_B:]).astype(BF16)
    cos = cos_ref[...]
    sin = sin_ref[...]
    q = _rope(_dot(xb, w_ref[:, 2 * WIDTH_A:2 * WIDTH_A + WIDTH_B]), cos, sin)
    q_ref[0] = (q * Q_SCALE).astype(BF16)
    k = _rope(_dot(xb, w_ref[:, 2 * WIDTH_A + WIDTH_B:2 * WIDTH_A + 2 * WIDTH_B]), cos, sin)
    k_ref[0] = k.astype(BF16)
    vt = _dot_nt(wvt_ref[...], xb).astype(BF16)
    ones = jnp.ones((V_EXT_ROWS - HEAD_DIM_B, MOBA_BLOCK), BF16)
    for t in range(PROJ_ROWS // MOBA_BLOCK):
        for h in range(N_HEADS_B):
            vext_ref[0, t, h * V_EXT_ROWS:h * V_EXT_ROWS + HEAD_DIM_B, :] = \
                vt[h * HEAD_DIM_B:(h + 1) * HEAD_DIM_B, t * MOBA_BLOCK:(t + 1) * MOBA_BLOCK]
            vext_ref[0, t, h * V_EXT_ROWS + HEAD_DIM_B:(h + 1) * V_EXT_ROWS, :] = ones

    vn = _layer_norm(gv, lng_ref[...], lnb_ref[...]).astype(BF16)
    t_idx = lax.broadcasted_iota(jnp.int32, (CHUNK, 2 * CHUNK), 0)
    s_idx = lax.broadcasted_iota(jnp.int32, (CHUNK, 2 * CHUNK), 1) % CHUNK
    causal = s_idx <= t_idx
    w_pairs = [jnp.where(causal, wsp_ref[p], jnp.zeros((), BF16)) for p in range(N_GROUPS_A // 2)]
    lane = lax.broadcasted_iota(jnp.int32, (CHUNK, LANES), 1)
    left = lane < GROUP_DIM_A
    zero = jnp.zeros((CHUNK, LANES), BF16)
    for c in range(PROJ_ROWS // CHUNK):
        rows = slice(c * CHUNK, (c + 1) * CHUNK)
        mixed = []
        for p in range(N_GROUPS_A // 2):
            blk = vn[rows, p * LANES:(p + 1) * LANES]
            stacked = jnp.concatenate([jnp.where(left, blk, zero), jnp.where(left, zero, blk)], axis=0)
            mixed.append(_dot(w_pairs[p], stacked))
        mixed = jnp.concatenate(mixed, axis=1) + bsp_ref[...]
        ya_ref[rows, :] = (gu[rows, :] * mixed).astype(BF16)


def _projection(x2, w_in_b, cos_t, sin_t, ln_g, ln_b, w_sp, b_sp, bsz, seq):
    n_rows = x2.shape[0]
    tiles_per_seq = seq // PROJ_ROWS
    blocks_per_tile = PROJ_ROWS // MOBA_BLOCK
    const = lambda i: (0, 0)
    resident = pl.Buffered(1)
    seq_tile = lambda i: (i // tiles_per_seq, i % tiles_per_seq, 0)
    return pl.pallas_call(
        _proj_kernel,
        grid=(n_rows // PROJ_ROWS,),
        in_specs=[
            pl.BlockSpec((PROJ_ROWS, D_MODEL), lambda i: (i, 0)),
            pl.BlockSpec((D_MODEL, PROJ_WIDTH), const, pipeline_mode=resident),
            pl.BlockSpec((PROJ_ROWS, LANES), lambda i: (i % tiles_per_seq, 0)),
            pl.BlockSpec((PROJ_ROWS, LANES), lambda i: (i % tiles_per_seq, 0)),
            pl.BlockSpec((1, WIDTH_A), const, pipeline_mode=resident),
            pl.BlockSpec((1, WIDTH_A), const, pipeline_mode=resident),
            pl.BlockSpec((N_GROUPS_A // 2, CHUNK, 2 * CHUNK), lambda i: (0, 0, 0),
                         pipeline_mode=resident),
            pl.BlockSpec((CHUNK, WIDTH_A), const, pipeline_mode=resident),
        ],
        out_specs=[
            pl.BlockSpec((PROJ_ROWS, WIDTH_A), lambda i: (i, 0)),
            pl.BlockSpec((1, PROJ_ROWS, WIDTH_B), seq_tile),
            pl.BlockSpec((1, PROJ_ROWS, WIDTH_B), seq_tile),
            pl.BlockSpec((1, blocks_per_tile, N_HEADS_B * V_EXT_ROWS, MOBA_BLOCK),
                         lambda i: (i // tiles_per_seq, i % tiles_per_seq, 0, 0)),
            pl.BlockSpec((PROJ_ROWS, 2 * D_MODEL), lambda i: (i, 0)),
        ],
        out_shape=[
            jax.ShapeDtypeStruct((n_rows, WIDTH_A), BF16),
            jax.ShapeDtypeStruct((bsz, seq, WIDTH_B), BF16),
            jax.ShapeDtypeStruct((bsz, seq, WIDTH_B), BF16),
            jax.ShapeDtypeStruct((bsz, seq // MOBA_BLOCK, N_HEADS_B * V_EXT_ROWS, MOBA_BLOCK), BF16),
            jax.ShapeDtypeStruct((n_rows, 2 * D_MODEL), BF16),
        ],
        scratch_shapes=[pltpu.VMEM((WIDTH_B, D_MODEL), BF16)],
        compiler_params=pltpu.CompilerParams(
            dimension_semantics=("arbitrary",), vmem_limit_bytes=VMEM_LIMIT),
        name="sgu_moba_projection",
    )(x2, w_in_b, cos_t, sin_t, ln_g, ln_b, w_sp, b_sp)


def _gate_kernel(q_ref, k_ref, pen_ref, *, n_blocks, seq):
    k_mean = jnp.sum(k_ref[0].astype(F32).reshape(n_blocks, MOBA_BLOCK, LANES), axis=1) \
        * np.float32(1.0 / MOBA_BLOCK)
    lane_k = lax.broadcasted_iota(jnp.int32, (n_blocks, LANES), 1)
    blk_row = lax.broadcasted_iota(jnp.int32, (n_blocks, GATE_CHUNK), 0)
    lane_t = lax.broadcasted_iota(jnp.int32, (n_blocks, GATE_CHUNK), 1)
    heads = LANES // HEAD_DIM_B
    km = [jnp.where(lane_k // HEAD_DIM_B == h, k_mean, 0.0) for h in range(heads)]
    km_hi = [m.astype(BF16) for m in km]
    km_lo = [(m - hi.astype(F32)).astype(BF16) for m, hi in zip(km, km_hi)]
    km_rows = jnp.concatenate(km_hi + km_lo, axis=0)
    for c in range(seq // GATE_CHUNK):
        q_c = q_ref[0, c * GATE_CHUNK:(c + 1) * GATE_CHUNK, :]
        gate_rows = _dot_nt(km_rows, q_c)
        for h in range(heads):
            gate = (gate_rows[h * n_blocks:(h + 1) * n_blocks, :]
                    + gate_rows[(heads + h) * n_blocks:(heads + h + 1) * n_blocks, :])
            past = blk_row < (lane_t + c * GATE_CHUNK) // MOBA_BLOCK
            gate = jnp.where(past, gate, NEG_INF)
            sel = jnp.zeros(gate.shape, jnp.bool_)
            for _ in range(MOBA_TOPK):
                best = jnp.max(gate, axis=0, keepdims=True)
                first = jnp.min(jnp.where(gate == best, blk_row, n_blocks), axis=0, keepdims=True)
                pick = blk_row == first
                sel = sel | pick
                gate = jnp.where(pick, NEG_INF, gate)
            pen_ref[0, h, :, c * GATE_CHUNK:(c + 1) * GATE_CHUNK] = \
                jnp.where(sel & past, 0.0, POS_INF).astype(F32)


def _moba_gate(q, k, bsz, seq):
    n_blocks = seq // MOBA_BLOCK
    return pl.pallas_call(
        functools.partial(_gate_kernel, n_blocks=n_blocks, seq=seq),
        grid=(bsz, WIDTH_B // LANES),
        in_specs=[
            pl.BlockSpec((1, seq, LANES), lambda b, p: (b, 0, p)),
            pl.BlockSpec((1, seq, LANES), lambda b, p: (b, 0, p)),
        ],
        out_specs=pl.BlockSpec((1, LANES // HEAD_DIM_B, n_blocks, seq), lambda b, p: (b, p, 0, 0)),
        out_shape=jax.ShapeDtypeStruct((bsz, N_HEADS_B, n_blocks, seq), F32),
        compiler_params=pltpu.CompilerParams(
            dimension_semantics=("arbitrary", "arbitrary"), vmem_limit_bytes=VMEM_LIMIT),
        name="moba_gate",
    )(q, k)


def _attn_kernel(q_ref, k_ref, vext_ref, pen_ref, o_ref, qm_ref, m_ref, acc_ref,
                 sa_ref, samax_ref, sb_ref, sbmax_ref):
    i = pl.program_id(1)
    pair_heads = LANES // HEAD_DIM_B
    n_pairs = N_HEADS_B // pair_heads

    row0 = pl.multiple_of(i * MOBA_BLOCK, MOBA_BLOCK)
    lane_q = lax.broadcasted_iota(jnp.int32, (MOBA_BLOCK, LANES), 1)
    key_pos = lax.broadcasted_iota(jnp.int32, (MOBA_BLOCK, MOBA_BLOCK), 0)
    qry_pos = lax.broadcasted_iota(jnp.int32, (MOBA_BLOCK, MOBA_BLOCK), 1)

    q_pairs = [q_ref[0, :, p * LANES:(p + 1) * LANES] for p in range(n_pairs)]
    k_own = [k_ref[0, pl.ds(row0, MOBA_BLOCK), p * LANES:(p + 1) * LANES] for p in range(n_pairs)]
    v_own = [vext_ref[0, i, h * V_EXT_ROWS:(h + 1) * V_EXT_ROWS, :] for h in range(N_HEADS_B)]
    qm_all, s_all = [], []
    for h in range(N_HEADS_B):
        in_head = lane_q // HEAD_DIM_B == h % pair_heads
        qm = jnp.where(in_head, q_pairs[h // pair_heads], jnp.zeros((), BF16))
        qm_all.append(qm)
        s_all.append(_dot_nt(k_own[h // pair_heads], qm))
    k_first = [k_ref[0, 0:MOBA_BLOCK, p * LANES:(p + 1) * LANES] for p in range(n_pairs)]
    m_all, acc_all, s0_all = [], [], []
    for h in range(N_HEADS_B):
        s0_all.append(_dot_nt(k_first[h // pair_heads], qm_all[h]))
        s = jnp.where(key_pos <= qry_pos, s_all[h], NEG_INF)
        m0 = jnp.max(s, axis=0, keepdims=True)
        m_all.append(m0)
        acc_all.append(_dot(v_own[h], jnp.exp2(s - m0).astype(BF16)))
    for h in range(N_HEADS_B):
        qm_ref[h] = qm_all[h]
        m_ref[h] = m_all[h]
        acc_ref[h] = acc_all[h]
        sa_ref[h] = s0_all[h]
        samax_ref[h] = jnp.max(s0_all[h], axis=0, keepdims=True)

    def load_keys(j):
        col0 = pl.multiple_of(j * MOBA_BLOCK, MOBA_BLOCK)
        return [k_ref[0, pl.ds(col0, MOBA_BLOCK), p * LANES:(p + 1) * LANES] for p in range(n_pairs)]

    def load_block(j):
        v_all = [vext_ref[0, j, h * V_EXT_ROWS:(h + 1) * V_EXT_ROWS, :] for h in range(N_HEADS_B)]
        pen_all = [pen_ref[0, h, pl.ds(j, 1), :] for h in range(N_HEADS_B)]
        return v_all, pen_all

    def absorb_head(h, v_h, pen_h, s_ref, smax_ref):
        m_run = m_ref[h]
        m_new = jnp.maximum(m_run, smax_ref[h] - pen_h)
        p_j = jnp.exp2(s_ref[h] - (m_new + pen_h)).astype(BF16)
        m_ref[h] = m_new
        acc_ref[h] = jnp.exp2(m_run - m_new) * acc_ref[h] + _dot(v_h, p_j)

    def fused(k_next, s_next_ref, smax_next_ref, block, s_ref, smax_ref):
        v_all, pen_all = block
        for h in range(N_HEADS_B):
            s_new = _dot_nt(k_next[h // pair_heads], qm_ref[h])
            s_next_ref[h] = s_new
            smax_next_ref[h] = jnp.max(s_new, axis=0, keepdims=True)
            absorb_head(h, v_all[h], pen_all[h], s_ref, smax_ref)

    def kv_pair(jj, carry):
        j = 2 * jj
        keys_b, keys_a = load_keys(j + 1), load_keys(j + 2)
        block_a, block_b = load_block(j), load_block(j + 1)
        fused(keys_b, sb_ref, sbmax_ref, block_a, sa_ref, samax_ref)
        fused(keys_a, sa_ref, samax_ref, block_b, sb_ref, sbmax_ref)
        return carry

    def kv_quad(jq, carry):
        kv_pair(2 * jq, carry)
        kv_pair(2 * jq + 1, carry)
        return carry

    def kv_oct(jo, carry):
        kv_quad(2 * jo, carry)
        kv_quad(2 * jo + 1, carry)
        return carry

    lax.fori_loop(0, i // 8, kv_oct, 0)
    lax.fori_loop(i // 8 * 2, i // 4, kv_quad, 0)
    lax.fori_loop(i // 4 * 2, i // 2, kv_pair, 0)

    @pl.when(i % 2 == 1)
    def _():
        v_all, pen_all = load_block(i - 1)
        for h in range(N_HEADS_B):
            absorb_head(h, v_all[h], pen_all[h], sa_ref, samax_ref)

    for h in range(N_HEADS_B):
        acc = acc_ref[h]
        out = acc[0:HEAD_DIM_B, :] / acc[HEAD_DIM_B:HEAD_DIM_B + 1, :]
        o_ref[0, 0, h * HEAD_DIM_B:(h + 1) * HEAD_DIM_B, :] = out.astype(BF16)


def _moba_attention(q, k, vext, pen, bsz, seq):
    n_blocks = seq // MOBA_BLOCK
    return pl.pallas_call(
        _attn_kernel,
        grid=(bsz, n_blocks),
        in_specs=[
            pl.BlockSpec((1, MOBA_BLOCK, WIDTH_B), lambda b, i: (b, i, 0)),
            pl.BlockSpec((1, seq, WIDTH_B), lambda b, i: (b, 0, 0)),
            pl.BlockSpec((1, n_blocks, N_HEADS_B * V_EXT_ROWS, MOBA_BLOCK), lambda b, i: (b, 0, 0, 0)),
            pl.BlockSpec((1, N_HEADS_B, n_blocks, MOBA_BLOCK), lambda b, i: (b, 0, 0, i)),
        ],
        out_specs=pl.BlockSpec((1, 1, WIDTH_B, MOBA_BLOCK), lambda b, i: (b, i, 0, 0)),
        out_shape=jax.ShapeDtypeStruct((bsz, n_blocks, WIDTH_B, MOBA_BLOCK), BF16),
        scratch_shapes=[
            pltpu.VMEM((N_HEADS_B, MOBA_BLOCK, LANES), BF16),
            pltpu.VMEM((N_HEADS_B, 1, MOBA_BLOCK), F32),
            pltpu.VMEM((N_HEADS_B, V_EXT_ROWS, MOBA_BLOCK), F32),
            pltpu.VMEM((N_HEADS_B, MOBA_BLOCK, MOBA_BLOCK), F32),
            pltpu.VMEM((N_HEADS_B, 1, MOBA_BLOCK), F32),
            pltpu.VMEM((N_HEADS_B, MOBA_BLOCK, MOBA_BLOCK), F32),
            pltpu.VMEM((N_HEADS_B, 1, MOBA_BLOCK), F32),
        ],
        compiler_params=pltpu.CompilerParams(
            dimension_semantics=("arbitrary", "arbitrary"), vmem_limit_bytes=VMEM_LIMIT),
        name="moba_attention",
    )(q, k, vext, pen)


def _merge_kernel(x_ref, ya_ref, ybt_ref, g_ref, bg_ref, wa_ref, wb_ref, wo_ref, lng_ref, lnb_ref,
                  o_ref, *, alpha):
    branch_a = _dot(ya_ref[...], wa_ref[...])
    branch_b = jnp.concatenate(
        [_dot_tn(ybt_ref[0, t], wb_ref[...]) for t in range(MERGE_ROWS // MOBA_BLOCK)], axis=0)
    g = g_ref[...].astype(F32) + bg_ref[...]
    merged = (jax.nn.sigmoid(g[:, :D_MODEL]) * branch_a
              + jax.nn.sigmoid(g[:, D_MODEL:]) * branch_b)
    mix = _dot(merged.astype(BF16), wo_ref[...])
    o_ref[...] = _layer_norm(alpha * x_ref[...] + mix, lng_ref[...], lnb_ref[...])


def _merge(x2, ya, ybt, g, b_gate, w_a, w_b, w_o, ln_g, ln_b, alpha, seq):
    n_rows = x2.shape[0]
    tiles_per_seq = seq // MERGE_ROWS
    const = lambda i: (0, 0)
    resident = pl.Buffered(1)
    return pl.pallas_call(
        functools.partial(_merge_kernel, alpha=alpha),
        grid=(n_rows // MERGE_ROWS,),
        in_specs=[
            pl.BlockSpec((MERGE_ROWS, D_MODEL), lambda i: (i, 0)),
            pl.BlockSpec((MERGE_ROWS, WIDTH_A), lambda i: (i, 0)),
            pl.BlockSpec((1, MERGE_ROWS // MOBA_BLOCK, WIDTH_B, MOBA_BLOCK),
                         lambda i: (i // tiles_per_seq, i % tiles_per_seq, 0, 0)),
            pl.BlockSpec((MERGE_ROWS, 2 * D_MODEL), lambda i: (i, 0)),
            pl.BlockSpec((1, 2 * D_MODEL), const, pipeline_mode=resident),
            pl.BlockSpec((WIDTH_A, D_MODEL), const, pipeline_mode=resident),
            pl.BlockSpec((WIDTH_B, D_MODEL), const, pipeline_mode=resident),
            pl.BlockSpec((D_MODEL, D_MODEL), const, pipeline_mode=resident),
            pl.BlockSpec((1, D_MODEL), const, pipeline_mode=resident),
            pl.BlockSpec((1, D_MODEL), const, pipeline_mode=resident),
        ],
        out_specs=pl.BlockSpec((MERGE_ROWS, D_MODEL), lambda i: (i, 0)),
        out_shape=jax.ShapeDtypeStruct((n_rows, D_MODEL), F32),
        compiler_params=pltpu.CompilerParams(
            dimension_semantics=("arbitrary",), vmem_limit_bytes=VMEM_LIMIT),
        name="merge_outproj_ln",
    )(x2, ya, ybt, g, b_gate, w_a, w_b, w_o, ln_g, ln_b)


def _ffn_kernel(x_ref, wup_ref, cw_ref, cb_ref, wdn_ref, lng_ref, lnb_ref, o_ref,
                hbuf_ref, carry_ref, acc_ref, *, alpha, tiles_per_seq):
    i = pl.program_id(0)

    @pl.when(i % tiles_per_seq == 0)
    def _():
        carry_ref[...] = jnp.zeros(carry_ref.shape, F32)

    x = x_ref[...]
    xb = x.astype(BF16)
    acc_ref[...] = jnp.zeros(acc_ref.shape, F32)
    pad = carry_ref.shape[0]
    n_steps = D_FF // FF_TILE

    def cols_of(j, half):
        c0 = half * D_FF + j * FF_TILE
        return slice(c0, c0 + FF_TILE)

    def up_proj(j):
        for half in range(2):
            cols = cols_of(j, half)
            hbuf = hbuf_ref.at[(2 * j + half) % hbuf_ref.shape[0]]
            hbuf[0:pad, :] = carry_ref[:, cols]
            h = _dot(xb, wup_ref[:, cols])
            hbuf[pad:, :] = h
            carry_ref[:, cols] = h[FFN_ROWS - pad:, :]

    def conv_act(j):
        conv = []
        for half in range(2):
            cols = cols_of(j, half)
            hbuf = hbuf_ref.at[(2 * j + half) % hbuf_ref.shape[0]]
            cw = cw_ref[:, cols]
            conv.append(cw[0:1, :] * hbuf[pad - 2:pad - 2 + FFN_ROWS, :]
                        + cw[1:2, :] * hbuf[pad - 1:pad - 1 + FFN_ROWS, :]
                        + cw[2:3, :] * hbuf[pad:, :] + cb_ref[:, cols])
        return (_gelu(conv[0]) * conv[1]).astype(BF16)

    for j in range(min(UP_LOOKAHEAD, n_steps)):
        up_proj(j)
    for j in range(n_steps):
        if j + UP_LOOKAHEAD < n_steps:
            up_proj(j + UP_LOOKAHEAD)
        acc_ref[...] += _dot(conv_act(j), wdn_ref[j * FF_TILE:(j + 1) * FF_TILE, :])
    o_ref[...] = _layer_norm(alpha * x + acc_ref[...], lng_ref[...], lnb_ref[...])


def _conv_ffn(x1, w_up, conv_w, conv_b, w_dn, ln_g, ln_b, alpha, seq):
    n_rows = x1.shape[0]
    tiles_per_seq = seq // FFN_ROWS
    const = lambda i: (0, 0)
    resident = pl.Buffered(1)
    pad = 8
    return pl.pallas_call(
        functools.partial(_ffn_kernel, alpha=alpha, tiles_per_seq=tiles_per_seq),
        grid=(n_rows // FFN_ROWS,),
        in_specs=[
            pl.BlockSpec((FFN_ROWS, D_MODEL), lambda i: (i, 0)),
            pl.BlockSpec((D_MODEL, 2 * D_FF), const, pipeline_mode=resident),
            pl.BlockSpec((CONV_WIDTH, 2 * D_FF), const, pipeline_mode=resident),
            pl.BlockSpec((1, 2 * D_FF), const, pipeline_mode=resident),
            pl.BlockSpec((D_FF, D_MODEL), const, pipeline_mode=resident),
            pl.BlockSpec((1, D_MODEL), const, pipeline_mode=resident),
            pl.BlockSpec((1, D_MODEL), const, pipeline_mode=resident),
        ],
        out_specs=pl.BlockSpec((FFN_ROWS, D_MODEL), lambda i: (i, 0)),
        out_shape=jax.ShapeDtypeStruct((n_rows, D_MODEL), F32),
        scratch_shapes=[
            pltpu.VMEM((2 * (UP_LOOKAHEAD + 1), pad + FFN_ROWS, FF_TILE), F32),
            pltpu.VMEM((pad, 2 * D_FF), F32),
            pltpu.VMEM((FFN_ROWS, D_MODEL), F32),
        ],
        compiler_params=pltpu.CompilerParams(
            dimension_semantics=("arbitrary",), vmem_limit_bytes=VMEM_LIMIT),
        name="conv_ffn_ln",
    )(x1, w_up, conv_w, conv_b, w_dn, ln_g, ln_b)


def _rope_tables(seq):
    half = HEAD_DIM_B // 2
    inv_freq = np.float32(ROPE_THETA) ** (-np.arange(half, dtype=np.float32) / np.float32(half))
    ang = np.arange(seq, dtype=np.float32)[:, None] * inv_freq[None, :]
    cos = np.tile(np.cos(ang), (1, LANES // half)).astype(np.float32)
    sin = np.tile(np.sin(ang), (1, LANES // half)).astype(np.float32)
    first_half = (np.arange(LANES) % HEAD_DIM_B) < half
    return jnp.asarray(cos), jnp.asarray(np.where(first_half[None, :], -sin, sin))


def kernel(x, w_in, b_gate, sgu_ln_g, sgu_ln_b, w_spatial, b_spatial, w_branch_a, w_branch_b,
           w_out, ln1_g, ln1_b, w_up, conv_w, conv_b, w_down, ln2_g, ln2_b):
    bsz, seq, d_model = x.shape
    depth = w_in.shape[0]
    assert d_model == D_MODEL and all(seq % t == 0 for t in (PROJ_ROWS, MERGE_ROWS, FFN_ROWS))
    assert all(t % MOBA_BLOCK == 0 for t in (PROJ_ROWS, MERGE_ROWS))
    alpha = float((2.0 * depth) ** 0.25)
    cos_t, sin_t = _rope_tables(seq)
    x2 = x.reshape(bsz * seq, d_model)
    for l in range(depth):
        w_in_b = w_in[l].astype(BF16)
        w_sp = w_spatial[l].astype(BF16).reshape(N_GROUPS_A // 2, 2, CHUNK, CHUNK)
        w_sp = w_sp.transpose(0, 2, 1, 3).reshape(N_GROUPS_A // 2, CHUNK, 2 * CHUNK)
        b_sp = jnp.repeat(b_spatial[l].T, GROUP_DIM_A, axis=1)
        ya, q, k, vext, g = _projection(
            x2, w_in_b, cos_t, sin_t, sgu_ln_g[l][None, :], sgu_ln_b[l][None, :],
            w_sp, b_sp, bsz, seq)
        pen = _moba_gate(q, k, bsz, seq)
        ybt = _moba_attention(q, k, vext, pen, bsz, seq)
        x1 = _merge(x2, ya, ybt, g, b_gate[l][None, :], w_branch_a[l].astype(BF16),
                    w_branch_b[l].astype(BF16), w_out[l].astype(BF16),
                    ln1_g[l][None, :], ln1_b[l][None, :], alpha, seq)
        x2 = _conv_ffn(x1, w_up[l].astype(BF16), conv_w[l], conv_b[l][None, :],
                       w_down[l].astype(BF16), ln2_g[l][None, :], ln2_b[l][None, :], alpha, seq)
    return x2.reshape(bsz, seq, d_model)
```

```python
import functools

import jax
import jax.numpy as jnp
import numpy as np
from jax import lax
from jax.experimental import pallas as pl
from jax.experimental.pallas import tpu as pltpu

F32 = jnp.float32
BF16 = jnp.bfloat16

D_MODEL = 1024
N_GROUPS_A = 8
GROUP_DIM_A = 64
WIDTH_A = N_GROUPS_A * GROUP_DIM_A
CHUNK = 128
N_HEADS_B = 8
HEAD_DIM_B = 64
WIDTH_B = N_HEADS_B * HEAD_DIM_B
MOBA_BLOCK = 256
MOBA_TOPK = 3
ROPE_THETA = 10000.0
D_FF = 2816
CONV_WIDTH = 3
LN_EPS = 1e-5
PROJ_WIDTH = 2 * WIDTH_A + 3 * WIDTH_B + 2 * D_MODEL

LANES = 128
PROJ_ROWS = 1024
MERGE_ROWS = 1024
FF_TILE = 256
FFN_ROWS = 512
UP_LOOKAHEAD = 3
GATE_CHUNK = 1024
V_EXT_ROWS = HEAD_DIM_B + 16
VMEM_LIMIT = 56 * 1024 * 1024
NEG_INF = float("-inf")
POS_INF = float("inf")
Q_SCALE = np.float32(HEAD_DIM_B ** -0.5 * np.log2(np.e))


def _gelu(x):
    return 0.5 * x * (1.0 + lax.erf(x * np.float32(np.sqrt(0.5))))


def _layer_norm(x, g, b):
    mu = jnp.mean(x, axis=-1, keepdims=True)
    xc = x - mu
    var = jnp.mean(xc * xc, axis=-1, keepdims=True)
    return xc * lax.rsqrt(var + LN_EPS) * g + b


def _dot(a, b):
    return jnp.dot(a, b, preferred_element_type=F32)


def _dot_nt(a, b):
    return lax.dot_general(a, b, (((1,), (1,)), ((), ())), preferred_element_type=F32)


def _dot_tn(a, b):
    return lax.dot_general(a, b, (((0,), (0,)), ((), ())), preferred_element_type=F32)


def _rope(t, cos, sin_signed):
    lane = lax.broadcasted_iota(jnp.int32, (t.shape[0], LANES), 1)
    first_half = (lane % HEAD_DIM_B) < (HEAD_DIM_B // 2)
    out = []
    for c in range(t.shape[1] // LANES):
        blk = t[:, c * LANES:(c + 1) * LANES]
        up = pltpu.roll(blk, LANES - HEAD_DIM_B // 2, 1)
        down = pltpu.roll(blk, HEAD_DIM_B // 2, 1)
        partner = jnp.where(first_half, up, down)
        out.append(blk * cos + partner * sin_signed)
    return jnp.concatenate(out, axis=1)


def _proj_kernel(x_ref, w_ref, cos_ref, sin_ref, lng_ref, lnb_ref, wsp_ref, bsp_ref,
                 ya_ref, q_ref, k_ref, vext_ref, g_ref, wvt_ref):
    @pl.when(pl.program_id(0) == 0)
    def _():
        w_v = w_ref[:, 2 * WIDTH_A + 2 * WIDTH_B:2 * WIDTH_A + 3 * WIDTH_B].astype(F32)
        wvt_ref[...] = w_v.T.astype(BF16)

    xb = x_ref[...].astype(BF16)
    gu = _gelu(_dot(xb, w_ref[:, 0:WIDTH_A]))
    gv = _gelu(_dot(xb, w_ref[:, WIDTH_A:2 * WIDTH_A]))
    g_ref[...] = _dot(xb, w_ref[:, 2 * WIDTH_A + 3 * WIDTH_B:]).astype(BF16)
    cos = cos_ref[...]
    sin = sin_ref[...]
    q = _rope(_dot(xb, w_ref[:, 2 * WIDTH_A:2 * WIDTH_A + WIDTH_B]), cos, sin)
    q_ref[0] = (q * Q_SCALE).astype(BF16)
    k = _rope(_dot(xb, w_ref[:, 2 * WIDTH_A + WIDTH_B:2 * WIDTH_A + 2 * WIDTH_B]), cos, sin)
    k_ref[0] = k.astype(BF16)
    vt = _dot_nt(wvt_ref[...], xb).astype(BF16)
    ones = jnp.ones((V_EXT_ROWS - HEAD_DIM_B, MOBA_BLOCK), BF16)
    for t in range(PROJ_ROWS // MOBA_BLOCK):
        for h in range(N_HEADS_B):
            vext_ref[0, t, h * V_EXT_ROWS:h * V_EXT_ROWS + HEAD_DIM_B, :] = \
                vt[h * HEAD_DIM_B:(h + 1) * HEAD_DIM_B, t * MOBA_BLOCK:(t + 1) * MOBA_BLOCK]
            vext_ref[0, t, h * V_EXT_ROWS + HEAD_DIM_B:(h + 1) * V_EXT_ROWS, :] = ones

    vn = _layer_norm(gv, lng_ref[...], lnb_ref[...]).astype(BF16)
    t_idx = lax.broadcasted_iota(jnp.int32, (CHUNK, 2 * CHUNK), 0)
    s_idx = lax.broadcasted_iota(jnp.int32, (CHUNK, 2 * CHUNK), 1) % CHUNK
    causal = s_idx <= t_idx
    w_pairs = [jnp.where(causal, wsp_ref[p], jnp.zeros((), BF16)) for p in range(N_GROUPS_A // 2)]
    lane = lax.broadcasted_iota(jnp.int32, (CHUNK, LANES), 1)
    left = lane < GROUP_DIM_A
    zero = jnp.zeros((CHUNK, LANES), BF16)
    for c in range(PROJ_ROWS // CHUNK):
        rows = slice(c * CHUNK, (c + 1) * CHUNK)
        mixed = []
        for p in range(N_GROUPS_A // 2):
            blk = vn[rows, p * LANES:(p + 1) * LANES]
            stacked = jnp.concatenate([jnp.where(left, blk, zero), jnp.where(left, zero, blk)], axis=0)
            mixed.append(_dot(w_pairs[p], stacked))
        mixed = jnp.concatenate(mixed, axis=1) + bsp_ref[...]
        ya_ref[rows, :] = (gu[rows, :] * mixed).astype(BF16)


def _projection(x2, w_in_b, cos_t, sin_t, ln_g, ln_b, w_sp, b_sp, bsz, seq):
    n_rows = x2.shape[0]
    tiles_per_seq = seq // PROJ_ROWS
    blocks_per_tile = PROJ_ROWS // MOBA_BLOCK
    const = lambda i: (0, 0)
    resident = pl.Buffered(1)
    seq_tile = lambda i: (i // tiles_per_seq, i % tiles_per_seq, 0)
    return pl.pallas_call(
        _proj_kernel,
        grid=(n_rows // PROJ_ROWS,),
        in_specs=[
            pl.BlockSpec((PROJ_ROWS, D_MODEL), lambda i: (i, 0)),
            pl.BlockSpec((D_MODEL, PROJ_WIDTH), const, pipeline_mode=resident),
            pl.BlockSpec((PROJ_ROWS, LANES), lambda i: (i % tiles_per_seq, 0)),
            pl.BlockSpec((PROJ_ROWS, LANES), lambda i: (i % tiles_per_seq, 0)),
            pl.BlockSpec((1, WIDTH_A), const, pipeline_mode=resident),
            pl.BlockSpec((1, WIDTH_A), const, pipeline_mode=resident),
            pl.BlockSpec((N_GROUPS_A // 2, CHUNK, 2 * CHUNK), lambda i: (0, 0, 0),
                         pipeline_mode=resident),
            pl.BlockSpec((CHUNK, WIDTH_A), const, pipeline_mode=resident),
        ],
        out_specs=[
            pl.BlockSpec((PROJ_ROWS, WIDTH_A), lambda i: (i, 0)),
            pl.BlockSpec((1, PROJ_ROWS, WIDTH_B), seq_tile),
            pl.BlockSpec((1, PROJ_ROWS, WIDTH_B), seq_tile),
            pl.BlockSpec((1, blocks_per_tile, N_HEADS_B * V_EXT_ROWS, MOBA_BLOCK),
                         lambda i: (i // tiles_per_seq, i % tiles_per_seq, 0, 0)),
            pl.BlockSpec((PROJ_ROWS, 2 * D_MODEL), lambda i: (i, 0)),
        ],
        out_shape=[
            jax.ShapeDtypeStruct((n_rows, WIDTH_A), BF16),
            jax.ShapeDtypeStruct((bsz, seq, WIDTH_B), BF16),
            jax.ShapeDtypeStruct((bsz, seq, WIDTH_B), BF16),
            jax.ShapeDtypeStruct((bsz, seq // MOBA_BLOCK, N_HEADS_B * V_EXT_ROWS, MOBA_BLOCK), BF16),
            jax.ShapeDtypeStruct((n_rows, 2 * D_MODEL), BF16),
        ],
        scratch_shapes=[pltpu.VMEM((WIDTH_B, D_MODEL), BF16)],
        compiler_params=pltpu.CompilerParams(
            dimension_semantics=("arbitrary",), vmem_limit_bytes=VMEM_LIMIT),
        name="sgu_moba_projection",
    )(x2, w_in_b, cos_t, sin_t, ln_g, ln_b, w_sp, b_sp)


def _gate_kernel(q_ref, k_ref, pen_ref, *, n_blocks, seq):
    k_mean = jnp.sum(k_ref[0].astype(F32).reshape(n_blocks, MOBA_BLOCK, LANES), axis=1) \
        * np.float32(1.0 / MOBA_BLOCK)
    lane_k = lax.broadcasted_iota(jnp.int32, (n_blocks, LANES), 1)
    blk_row = lax.broadcasted_iota(jnp.int32, (n_blocks, GATE_CHUNK), 0)
    lane_t = lax.broadcasted_iota(jnp.int32, (n_blocks, GATE_CHUNK), 1)
    heads = LANES // HEAD_DIM_B
    km = [jnp.where(lane_k // HEAD_DIM_B == h, k_mean, 0.0) for h in range(heads)]
    km_hi = [m.astype(BF16) for m in km]
    km_lo = [(m - hi.astype(F32)).astype(BF16) for m, hi in zip(km, km_hi)]
    km_rows = jnp.concatenate(km_hi + km_lo, axis=0)
    for c in range(seq // GATE_CHUNK):
        q_c = q_ref[0, c * GATE_CHUNK:(c + 1) * GATE_CHUNK, :]
        gate_rows = _dot_nt(km_rows, q_c)
        for h in range(heads):
            gate = (gate_rows[h * n_blocks:(h + 1) * n_blocks, :]
                    + gate_rows[(heads + h) * n_blocks:(heads + h + 1) * n_blocks, :])
            past = blk_row < (lane_t + c * GATE_CHUNK) // MOBA_BLOCK
            gate = jnp.where(past, gate, NEG_INF)
            sel = jnp.zeros(gate.shape, jnp.bool_)
            for _ in range(MOBA_TOPK):
                best = jnp.max(gate, axis=0, keepdims=True)
                first = jnp.min(jnp.where(gate == best, blk_row, n_blocks), axis=0, keepdims=True)
                pick = blk_row == first
                sel = sel | pick
                gate = jnp.where(pick, NEG_INF, gate)
            pen_ref[0, h, :, c * GATE_CHUNK:(c + 1) * GATE_CHUNK] = \
                jnp.where(sel & past, 0.0, POS_INF).astype(F32)


def _moba_gate(q, k, bsz, seq):
    n_blocks = seq // MOBA_BLOCK
    return pl.pallas_call(
        functools.partial(_gate_kernel, n_blocks=n_blocks, seq=seq),
        grid=(bsz, WIDTH_B // LANES),
        in_specs=[
            pl.BlockSpec((1, seq, LANES), lambda b, p: (b, 0, p)),
            pl.BlockSpec((1, seq, LANES), lambda b, p: (b, 0, p)),
        ],
        out_specs=pl.BlockSpec((1, LANES // HEAD_DIM_B, n_blocks, seq), lambda b, p: (b, p, 0, 0)),
        out_shape=jax.ShapeDtypeStruct((bsz, N_HEADS_B, n_blocks, seq), F32),
        compiler_params=pltpu.CompilerParams(
            dimension_semantics=("arbitrary", "arbitrary"), vmem_limit_bytes=VMEM_LIMIT),
        name="moba_gate",
    )(q, k)


def _attn_kernel(q_ref, k_ref, vext_ref, pen_ref, o_ref, qm_ref, m_ref, acc_ref,
                 sa_ref, samax_ref, sb_ref, sbmax_ref):
    i = pl.program_id(1)
    pair_heads = LANES // HEAD_DIM_B
    n_pairs = N_HEADS_B // pair_heads

    row0 = pl.multiple_of(i * MOBA_BLOCK, MOBA_BLOCK)
    lane_q = lax.broadcasted_iota(jnp.int32, (MOBA_BLOCK, LANES), 1)
    key_pos = lax.broadcasted_iota(jnp.int32, (MOBA_BLOCK, MOBA_BLOCK), 0)
    qry_pos = lax.broadcasted_iota(jnp.int32, (MOBA_BLOCK, MOBA_BLOCK), 1)

    q_pairs = [q_ref[0, :, p * LANES:(p + 1) * LANES] for p in range(n_pairs)]
    k_own = [k_ref[0, pl.ds(row0, MOBA_BLOCK), p * LANES:(p + 1) * LANES] for p in range(n_pairs)]
    v_own = [vext_ref[0, i, h * V_EXT_ROWS:(h + 1) * V_EXT_ROWS, :] for h in range(N_HEADS_B)]
    qm_all, s_all = [], []
    for h in range(N_HEADS_B):
        in_head = lane_q // HEAD_DIM_B == h % pair_heads
        qm = jnp.where(in_head, q_pairs[h // pair_heads], jnp.zeros((), BF16))
        qm_all.append(qm)
        s_all.append(_dot_nt(k_own[h // pair_heads], qm))
    k_first = [k_ref[0, 0:MOBA_BLOCK, p * LANES:(p + 1) * LANES] for p in range(n_pairs)]
    m_all, acc_all, s0_all = [], [], []
    for h in range(N_HEADS_B):
        s0_all.append(_dot_nt(k_first[h // pair_heads], qm_all[h]))
        s = jnp.where(key_pos <= qry_pos, s_all[h], NEG_INF)
        m0 = jnp.max(s, axis=0, keepdims=True)
        m_all.append(m0)
        acc_all.append(_dot(v_own[h], jnp.exp2(s - m0).astype(BF16)))
    for h in range(N_HEADS_B):
        qm_ref[h] = qm_all[h]
        m_ref[h] = m_all[h]
        acc_ref[h] = acc_all[h]
        sa_ref[h] = s0_all[h]
        samax_ref[h] = jnp.max(s0_all[h], axis=0, keepdims=True)

    def load_keys(j):
        col0 = pl.multiple_of(j * MOBA_BLOCK, MOBA_BLOCK)
        return [k_ref[0, pl.ds(col0, MOBA_BLOCK), p * LANES:(p + 1) * LANES] for p in range(n_pairs)]

    def load_block(j):
        v_all = [vext_ref[0, j, h * V_EXT_ROWS:(h + 1) * V_EXT_ROWS, :] for h in range(N_HEADS_B)]
        pen_all = [pen_ref[0, h, pl.ds(j, 1), :] for h in range(N_HEADS_B)]
        return v_all, pen_all

    def absorb_head(h, v_h, pen_h, s_ref, smax_ref):
        m_run = m_ref[h]
        m_new = jnp.maximum(m_run, smax_ref[h] - pen_h)
        p_j = jnp.exp2(s_ref[h] - (m_new + pen_h)).astype(BF16)
        m_ref[h] = m_new
        acc_ref[h] = jnp.exp2(m_run - m_new) * acc_ref[h] + _dot(v_h, p_j)

    def fused(k_next, s_next_ref, smax_next_ref, block, s_ref, smax_ref):
        v_all, pen_all = block
        for h in range(N_HEADS_B):
            s_new = _dot_nt(k_next[h // pair_heads], qm_ref[h])
            s_next_ref[h] = s_new
            smax_next_ref[h] = jnp.max(s_new, axis=0, keepdims=True)
            absorb_head(h, v_all[h], pen_all[h], s_ref, smax_ref)

    def kv_pair(jj, carry):
        j = 2 * jj
        keys_b, keys_a = load_keys(j + 1), load_keys(j + 2)
        block_a, block_b = load_block(j), load_block(j + 1)
        fused(keys_b, sb_ref, sbmax_ref, block_a, sa_ref, samax_ref)
        fused(keys_a, sa_ref, samax_ref, block_b, sb_ref, sbmax_ref)
        return carry

    def kv_quad(jq, carry):
        kv_pair(2 * jq, carry)
        kv_pair(2 * jq + 1, carry)
        return carry

    def kv_oct(jo, carry):
        kv_quad(2 * jo, carry)
        kv_quad(2 * jo + 1, carry)
        return carry

    lax.fori_loop(0, i // 8, kv_oct, 0)
    lax.fori_loop(i // 8 * 2, i // 4, kv_quad, 0)
    lax.fori_loop(i // 4 * 2, i // 2, kv_pair, 0)

    @pl.when(i % 2 == 1)
    def _():
        v_all, pen_all = load_block(i - 1)
        for h in range(N_HEADS_B):
            absorb_head(h, v_all[h], pen_all[h], sa_ref, samax_ref)

    for h in range(N_HEADS_B):
        acc = acc_ref[h]
        out = acc[0:HEAD_DIM_B, :] / acc[HEAD_DIM_B:HEAD_DIM_B + 1, :]
        o_ref[0, 0, h * HEAD_DIM_B:(h + 1) * HEAD_DIM_B, :] = out.astype(BF16)


def _moba_attention(q, k, vext, pen, bsz, seq):
    n_blocks = seq // MOBA_BLOCK
    return pl.pallas_call(
        _attn_kernel,
        grid=(bsz, n_blocks),
        in_specs=[
            pl.BlockSpec((1, MOBA_BLOCK, WIDTH_B), lambda b, i: (b, i, 0)),
            pl.BlockSpec((1, seq, WIDTH_B), lambda b, i: (b, 0, 0)),
            pl.BlockSpec((1, n_blocks, N_HEADS_B * V_EXT_ROWS, MOBA_BLOCK), lambda b, i: (b, 0, 0, 0)),
            pl.BlockSpec((1, N_HEADS_B, n_blocks, MOBA_BLOCK), lambda b, i: (b, 0, 0, i)),
        ],
        out_specs=pl.BlockSpec((1, 1, WIDTH_B, MOBA_BLOCK), lambda b, i: (b, i, 0, 0)),
        out_shape=jax.ShapeDtypeStruct((bsz, n_blocks, WIDTH_B, MOBA_BLOCK), BF16),
        scratch_shapes=[
            pltpu.VMEM((N_HEADS_B, MOBA_BLOCK, LANES), BF16),
            pltpu.VMEM((N_HEADS_B, 1, MOBA_BLOCK), F32),
            pltpu.VMEM((N_HEADS_B, V_EXT_ROWS, MOBA_BLOCK), F32),
            pltpu.VMEM((N_HEADS_B, MOBA_BLOCK, MOBA_BLOCK), F32),
            pltpu.VMEM((N_HEADS_B, 1, MOBA_BLOCK), F32),
            pltpu.VMEM((N_HEADS_B, MOBA_BLOCK, MOBA_BLOCK), F32),
            pltpu.VMEM((N_HEADS_B, 1, MOBA_BLOCK), F32),
        ],
        compiler_params=pltpu.CompilerParams(
            dimension_semantics=("arbitrary", "arbitrary"), vmem_limit_bytes=VMEM_LIMIT),
        name="moba_attention",
    )(q, k, vext, pen)


def _merge_kernel(x_ref, ya_ref, ybt_ref, g_ref, bg_ref, wa_ref, wb_ref, wo_ref, lng_ref, lnb_ref,
                  o_ref, *, alpha):
    branch_a = _dot(ya_ref[...], wa_ref[...])
    branch_b = jnp.concatenate(
        [_dot_tn(ybt_ref[0, t], wb_ref[...]) for t in range(MERGE_ROWS // MOBA_BLOCK)], axis=0)
    g = g_ref[...].astype(F32) + bg_ref[...]
    merged = (jax.nn.sigmoid(g[:, :D_MODEL]) * branch_a
              + jax.nn.sigmoid(g[:, D_MODEL:]) * branch_b)
    mix = _dot(merged.astype(BF16), wo_ref[...])
    o_ref[...] = _layer_norm(alpha * x_ref[...] + mix, lng_ref[...], lnb_ref[...])


def _merge(x2, ya, ybt, g, b_gate, w_a, w_b, w_o, ln_g, ln_b, alpha, seq):
    n_rows = x2.shape[0]
    tiles_per_seq = seq // MERGE_ROWS
    const = lambda i: (0, 0)
    resident = pl.Buffered(1)
    return pl.pallas_call(
        functools.partial(_merge_kernel, alpha=alpha),
        grid=(n_rows // MERGE_ROWS,),
        in_specs=[
            pl.BlockSpec((MERGE_ROWS, D_MODEL), lambda i: (i, 0)),
            pl.BlockSpec((MERGE_ROWS, WIDTH_A), lambda i: (i, 0)),
            pl.BlockSpec((1, MERGE_ROWS // MOBA_BLOCK, WIDTH_B, MOBA_BLOCK),
                         lambda i: (i // tiles_per_seq, i % tiles_per_seq, 0, 0)),
            pl.BlockSpec((MERGE_ROWS, 2 * D_MODEL), lambda i: (i, 0)),
            pl.BlockSpec((1, 2 * D_MODEL), const, pipeline_mode=resident),
            pl.BlockSpec((WIDTH_A, D_MODEL), const, pipeline_mode=resident),
            pl.BlockSpec((WIDTH_B, D_MODEL), const, pipeline_mode=resident),
            pl.BlockSpec((D_MODEL, D_MODEL), const, pipeline_mode=resident),
            pl.BlockSpec((1, D_MODEL), const, pipeline_mode=resident),
            pl.BlockSpec((1, D_MODEL), const, pipeline_mode=resident),
        ],
        out_specs=pl.BlockSpec((MERGE_ROWS, D_MODEL), lambda i: (i, 0)),
        out_shape=jax.ShapeDtypeStruct((n_rows, D_MODEL), F32),
        compiler_params=pltpu.CompilerParams(
            dimension_semantics=("arbitrary",), vmem_limit_bytes=VMEM_LIMIT),
        name="merge_outproj_ln",
    )(x2, ya, ybt, g, b_gate, w_a, w_b, w_o, ln_g, ln_b)


def _ffn_kernel(x_ref, wup_ref, cw_ref, cb_ref, wdn_ref, lng_ref, lnb_ref, o_ref,
                hbuf_ref, carry_ref, acc_ref, *, alpha, tiles_per_seq):
    i = pl.program_id(0)

    @pl.when(i % tiles_per_seq == 0)
    def _():
        carry_ref[...] = jnp.zeros(carry_ref.shape, F32)

    x = x_ref[...]
    xb = x.astype(BF16)
    acc_ref[...] = jnp.zeros(acc_ref.shape, F32)
    pad = carry_ref.shape[0]
    n_steps = D_FF // FF_TILE

    def cols_of(j, half):
        c0 = half * D_FF + j * FF_TILE
        return slice(c0, c0 + FF_TILE)

    def up_proj(j):
        for half in range(2):
            cols = cols_of(j, half)
            hbuf = hbuf_ref.at[(2 * j + half) % hbuf_ref.shape[0]]
            hbuf[0:pad, :] = carry_ref[:, cols]
            h = _dot(xb, wup_ref[:, cols])
            hbuf[pad:, :] = h
            carry_ref[:, cols] = h[FFN_ROWS - pad:, :]

    def conv_act(j):
        conv = []
        for half in range(2):
            cols = cols_of(j, half)
            hbuf = hbuf_ref.at[(2 * j + half) % hbuf_ref.shape[0]]
            cw = cw_ref[:, cols]
            conv.append(cw[0:1, :] * hbuf[pad - 2:pad - 2 + FFN_ROWS, :]
                        + cw[1:2, :] * hbuf[pad - 1:pad - 1 + FFN_ROWS, :]
                        + cw[2:3, :] * hbuf[pad:, :] + cb_ref[:, cols])
        return (_gelu(conv[0]) * conv[1]).astype(BF16)

    for j in range(min(UP_LOOKAHEAD, n_steps)):
        up_proj(j)
    for j in range(n_steps):
        if j + UP_LOOKAHEAD < n_steps:
            up_proj(j + UP_LOOKAHEAD)
        acc_ref[...] += _dot(conv_act(j), wdn_ref[j * FF_TILE:(j + 1) * FF_TILE, :])
    o_ref[...] = _layer_norm(alpha * x + acc_ref[...], lng_ref[...], lnb_ref[...])


def _conv_ffn(x1, w_up, conv_w, conv_b, w_dn, ln_g, ln_b, alpha, seq):
    n_rows = x1.shape[0]
    tiles_per_seq = seq // FFN_ROWS
    const = lambda i: (0, 0)
    resident = pl.Buffered(1)
    pad = 8
    return pl.pallas_call(
        functools.partial(_ffn_kernel, alpha=alpha, tiles_per_seq=tiles_per_seq),
        grid=(n_rows // FFN_ROWS,),
        in_specs=[
            pl.BlockSpec((FFN_ROWS, D_MODEL), lambda i: (i, 0)),
            pl.BlockSpec((D_MODEL, 2 * D_FF), const, pipeline_mode=resident),
            pl.BlockSpec((CONV_WIDTH, 2 * D_FF), const, pipeline_mode=resident),
            pl.BlockSpec((1, 2 * D_FF), const, pipeline_mode=resident),
            pl.BlockSpec((D_FF, D_MODEL), const, pipeline_mode=resident),
            pl.BlockSpec((1, D_MODEL), const, pipeline_mode=resident),
            pl.BlockSpec((1, D_MODEL), const, pipeline_mode=resident),
        ],
        out_specs=pl.BlockSpec((FFN_ROWS, D_MODEL), lambda i: (i, 0)),
        out_shape=jax.ShapeDtypeStruct((n_rows, D_MODEL), F32),
        scratch_shapes=[
            pltpu.VMEM((2 * (UP_LOOKAHEAD + 1), pad + FFN_ROWS, FF_TILE), F32),
            pltpu.VMEM((pad, 2 * D_FF), F32),
            pltpu.VMEM((FFN_ROWS, D_MODEL), F32),
        ],
        compiler_params=pltpu.CompilerParams(
            dimension_semantics=("arbitrary",), vmem_limit_bytes=VMEM_LIMIT),
        name="conv_ffn_ln",
    )(x1, w_up, conv_w, conv_b, w_dn, ln_g, ln_b)


def _rope_tables(seq):
    half = HEAD_DIM_B // 2
    inv_freq = np.float32(ROPE_THETA) ** (-np.arange(half, dtype=np.float32) / np.float32(half))
    ang = np.arange(seq, dtype=np.float32)[:, None] * inv_freq[None, :]
    cos = np.tile(np.cos(ang), (1, LANES // half)).astype(np.float32)
    sin = np.tile(np.sin(ang), (1, LANES // half)).astype(np.float32)
    first_half = (np.arange(LANES) % HEAD_DIM_B) < half
    return jnp.asarray(cos), jnp.asarray(np.where(first_half[None, :], -sin, sin))


def kernel(x, w_in, b_gate, sgu_ln_g, sgu_ln_b, w_spatial, b_spatial, w_branch_a, w_branch_b,
           w_out, ln1_g, ln1_b, w_up, conv_w, conv_b, w_down, ln2_g, ln2_b):
    bsz, seq, d_model = x.shape
    depth = w_in.shape[0]
    assert d_model == D_MODEL and all(seq % t == 0 for t in (PROJ_ROWS, MERGE_ROWS, FFN_ROWS))
    assert all(t % MOBA_BLOCK == 0 for t in (PROJ_ROWS, MERGE_ROWS))
    alpha = float((2.0 * depth) ** 0.25)
    cos_t, sin_t = _rope_tables(seq)
    x2 = x.reshape(bsz * seq, d_model)
    for l in range(depth):
        w_in_b = w_in[l].astype(BF16)
        w_sp = w_spatial[l].astype(BF16).reshape(N_GROUPS_A // 2, 2, CHUNK, CHUNK)
        w_sp = w_sp.transpose(0, 2, 1, 3).reshape(N_GROUPS_A // 2, CHUNK, 2 * CHUNK)
        b_sp = jnp.repeat(b_spatial[l].T, GROUP_DIM_A, axis=1)
        ya, q, k, vext, g = _projection(
            x2, w_in_b, cos_t, sin_t, sgu_ln_g[l][None, :], sgu_ln_b[l][None, :],
            w_sp, b_sp, bsz, seq)
        pen = _moba_gate(q, k, bsz, seq)
        ybt = _moba_attention(q, k, vext, pen, bsz, seq)
        x1 = _merge(x2, ya, ybt, g, b_gate[l][None, :], w_branch_a[l].astype(BF16),
                    w_branch_b[l].astype(BF16), w_out[l].astype(BF16),
                    ln1_g[l][None, :], ln1_b[l][None, :], alpha, seq)
        x2 = _conv_ffn(x1, w_up[l].astype(BF16), conv_w[l], conv_b[l][None, :],
                       w_down[l].astype(BF16), ln2_g[l][None, :], ln2_b[l][None, :], alpha, seq)
    return x2.reshape(bsz, seq, d_model)
```
